```python
import math
import jax, jax.numpy as jnp
from jax import lax
import numpy as np

D_MODEL = 2048
BATCH = 8
SEQ = 2048
DEPTH = 1
DEC_BATCH = 4
DEC_SEQ = 2048
PAST_LEN = 128

MIX_WIDTH = D_MODEL
DIFF_WIDTH = D_MODEL // 2
MLA_WIDTH = D_MODEL - DIFF_WIDTH
DIFF_HEAD_DIM = 128
DIFF_HEADS = DIFF_WIDTH // (2 * DIFF_HEAD_DIM)
MLA_NOPE = 128
MLA_ROPE = 64
MLA_V = 128
MLA_HEADS = MLA_WIDTH // MLA_V
Q_LORA = 512
KV_LORA = 256
ROPE_THETA = 10000.0
IN_WIDTH = 3 * DIFF_WIDTH + Q_LORA + KV_LORA + MLA_ROPE
N_EXPERTS = 32
TOP_K = 4
D_FF = D_MODEL
SWIGLU_LIMIT = 7.0
SWIGLU_ALPHA = 1.702
MOE_BLOCK = 128
Q_BLOCK = 128
LN_EPS = 1e-5
RMS_EPS = 1e-6
DEEPNORM_ALPHA = (2.0 * DEPTH) ** 0.25
DEEPNORM_BETA = (8.0 * DEPTH) ** -0.25

kernel_name = "hymba_diffattn_mla_moe_encoder"


def _layernorm(x, g=None, b=None):
    xf = x.astype(jnp.float32)
    mu = jnp.mean(xf, axis=-1, keepdims=True)
    var = jnp.mean(jnp.square(xf - mu), axis=-1, keepdims=True)
    y = (xf - mu) * lax.rsqrt(var + LN_EPS)
    if g is not None:
        y = y * g.astype(jnp.float32) + b.astype(jnp.float32)
    return y.astype(x.dtype)


def _rmsnorm(x, g):
    xf = x.astype(jnp.float32)
    y = xf * lax.rsqrt(jnp.mean(jnp.square(xf), axis=-1, keepdims=True) + RMS_EPS)
    return (y * g.astype(jnp.float32)).astype(x.dtype)


def _rope(x, pos):
    d = x.shape[-1]
    inv = 1.0 / (ROPE_THETA ** (jnp.arange(0, d, 2, dtype=jnp.float32) / d))
    ang = pos.astype(jnp.float32)[:, None] * inv[None, :]
    shape = (ang.shape[0],) + (1,) * (x.ndim - 3) + (d // 2,)
    cos = jnp.cos(ang).reshape(shape).astype(x.dtype)
    sin = jnp.sin(ang).reshape(shape).astype(x.dtype)
    x1, x2 = x[..., : d // 2], x[..., d // 2:]
    return jnp.concatenate([x1 * cos - x2 * sin, x1 * sin + x2 * cos], axis=-1)


def _alibi_slopes(n_heads):
    return 2.0 ** (-8.0 * jnp.arange(1, n_heads + 1, dtype=jnp.float32) / n_heads)


def _diff_attention(q, k, v, lam, slopes):
    B, S, H, _, d = q.shape
    nq = S // Q_BLOCK
    scale = d ** -0.5
    q_blocks = q.reshape(B, nq, Q_BLOCK, H, 2, d).transpose(1, 0, 2, 3, 4, 5)
    kpos = jnp.arange(S)
    qpos = kpos.reshape(nq, Q_BLOCK)

    def one_block(args):
        qb, pos = args
        s = jnp.einsum('bqhmd,bkhmd->bhmqk', qb, k).astype(jnp.float32) * scale
        dist = jnp.abs(pos[:, None] - kpos[None, :]).astype(jnp.float32)
        s = s - slopes[None, :, None, None, None] * dist[None, None, None]
        p = jax.nn.softmax(s, axis=-1)
        a = p[:, :, 0] - lam * p[:, :, 1]
        return jnp.einsum('bhqk,bkhe->bqhe', a.astype(v.dtype), v)

    o = lax.map(one_block, (q_blocks, qpos))
    return o.transpose(1, 0, 2, 3, 4).reshape(B, S, H, v.shape[-1])


def _mla_attention(q_nope, q_pe, k_nope, k_pe, v):
    B, S, H, _ = q_nope.shape
    nq = S // Q_BLOCK
    scale = (MLA_NOPE + MLA_ROPE) ** -0.5
    qn_blocks = q_nope.reshape(B, nq, Q_BLOCK, H, -1).transpose(1, 0, 2, 3, 4)
    qp_blocks = q_pe.reshape(B, nq, Q_BLOCK, H, -1).transpose(1, 0, 2, 3, 4)

    def one_block(args):
        qn, qp = args
        s = (jnp.einsum('bqhd,bkhd->bhqk', qn, k_nope)
             + jnp.einsum('bqhr,bkr->bhqk', qp, k_pe)).astype(jnp.float32) * scale
        p = jax.nn.softmax(s, axis=-1)
        return jnp.einsum('bhqk,bkhe->bqhe', p.astype(v.dtype), v)

    o = lax.map(one_block, (qn_blocks, qp_blocks))
    return o.transpose(1, 0, 2, 3, 4).reshape(B, S, H * v.shape[-1])


def _mixer(h, lp, layer_idx):
    B, S, _ = h.shape
    proj = h @ lp['w_in']
    o1 = DIFF_WIDTH
    o2 = 2 * DIFF_WIDTH
    o3 = 3 * DIFF_WIDTH
    o4 = o3 + Q_LORA
    o5 = o4 + KV_LORA
    dq = proj[..., :o1].reshape(B, S, DIFF_HEADS, 2, DIFF_HEAD_DIM)
    dk = proj[..., o1:o2].reshape(B, S, DIFF_HEADS, 2, DIFF_HEAD_DIM)
    dv = proj[..., o2:o3].reshape(B, S, DIFF_HEADS, 2 * DIFF_HEAD_DIM)
    lam_init = 0.8 - 0.6 * math.exp(-0.3 * layer_idx)
    f32 = jnp.float32
    lam = (jnp.exp(jnp.sum(lp['diff_lambda_q1'].astype(f32) * lp['diff_lambda_k1'].astype(f32)))
           - jnp.exp(jnp.sum(lp['diff_lambda_q2'].astype(f32) * lp['diff_lambda_k2'].astype(f32)))
           + lam_init)
    o_diff = _diff_attention(dq, dk, dv, lam, _alibi_slopes(DIFF_HEADS))
    o_diff = (_rmsnorm(o_diff, lp['diff_subln_g']) * (1.0 - lam_init)).reshape(B, S, DIFF_WIDTH)
    pos = jnp.arange(S)
    cq = _rmsnorm(proj[..., o3:o4], lp['mla_q_norm_g'])
    q = (cq @ lp['mla_w_q_up']).reshape(B, S, MLA_HEADS, MLA_NOPE + MLA_ROPE)
    q_nope = q[..., :MLA_NOPE]
    q_pe = _rope(q[..., MLA_NOPE:], pos)
    ckv = _rmsnorm(proj[..., o4:o5], lp['mla_kv_norm_g'])
    kv = (ckv @ lp['mla_w_kv_up']).reshape(B, S, MLA_HEADS, MLA_NOPE + MLA_V)
    k_nope = kv[..., :MLA_NOPE]
    v = kv[..., MLA_NOPE:]
    k_pe = _rope(proj[..., o5:], pos)
    o_mla = _mla_attention(q_nope, q_pe, k_nope, k_pe, v)
    return jnp.concatenate([o_diff, o_mla], axis=-1) @ lp['w_out']


def _moe(h, w_router, b_router, w_mlp1, b_mlp1, w_mlp2, b_mlp2):
    B, S, D = h.shape
    N = B * S
    x = h.reshape(N, D)
    logits = (x @ w_router).astype(jnp.float32) + b_router.astype(jnp.float32)
    top_vals, top_idx = lax.top_k(logits, TOP_K)
    gates = jax.nn.softmax(top_vals, axis=-1).astype(h.dtype)
    n_assign = N * TOP_K
    flat_e = top_idx.reshape(-1).astype(jnp.int32)
    flat_tok = (jnp.arange(n_assign, dtype=jnp.int32) // TOP_K)
    flat_gate = gates.reshape(-1)
    order = jnp.argsort(flat_e)
    sorted_e = flat_e[order]
    counts = jnp.bincount(flat_e, length=N_EXPERTS)
    padded = (counts + MOE_BLOCK - 1) // MOE_BLOCK * MOE_BLOCK
    starts = jnp.cumsum(counts) - counts
    pends = jnp.cumsum(padded)
    pstarts = pends - padded
    dest = pstarts[sorted_e] + jnp.arange(n_assign, dtype=jnp.int32) - starts[sorted_e]
    n_blocks = -(-n_assign // MOE_BLOCK) + N_EXPERTS
    P = n_blocks * MOE_BLOCK
    buf_tok = jnp.full((P,), N, jnp.int32).at[dest].set(flat_tok[order])
    buf_gate = jnp.zeros((P,), h.dtype).at[dest].set(flat_gate[order])
    block_e = jnp.minimum(
        jnp.searchsorted(pends, jnp.arange(n_blocks, dtype=jnp.int32) * MOE_BLOCK, side='right'),
        N_EXPERTS - 1)
    x_pad = jnp.concatenate([x, jnp.zeros((1, D), x.dtype)], axis=0)

    def expert_block(args):
        tok, g, e = args
        hb = x_pad[tok] @ w_mlp1[e] + b_mlp1[e]
        glu = jnp.minimum(hb[:, ::2], SWIGLU_LIMIT)
        lin = jnp.clip(hb[:, 1::2], -SWIGLU_LIMIT, SWIGLU_LIMIT)
        act = glu * jax.nn.sigmoid(SWIGLU_ALPHA * glu) * (lin + 1.0)
        return (act @ w_mlp2[e] + b_mlp2[e]) * g[:, None]

    out = lax.map(expert_block, (buf_tok.reshape(n_blocks, MOE_BLOCK),
                                 buf_gate.reshape(n_blocks, MOE_BLOCK), block_e))
    y = jax.ops.segment_sum(out.reshape(P, D), buf_tok, num_segments=N + 1)[:N]
    return y.reshape(B, S, D)


def _layer(x, c, lp, layer_idx):
    mod = jax.nn.silu(c) @ lp['w_ada'] + lp['b_ada']
    shift_a, scale_a, gate_a, shift_m, scale_m, gate_m = jnp.split(mod[:, None, :], 6, axis=-1)
    h = _layernorm(x) * (1.0 + scale_a) + shift_a
    x = _layernorm(DEEPNORM_ALPHA * x + gate_a * _mixer(h, lp, layer_idx), lp['ln1_g'], lp['ln1_b'])
    h = _layernorm(x) * (1.0 + scale_m) + shift_m
    y = _moe(h, lp['w_router'], lp['b_router'], lp['w_mlp1'], lp['b_mlp1'], lp['w_mlp2'], lp['b_mlp2'])
    return _layernorm(DEEPNORM_ALPHA * x + gate_m * y, lp['ln2_g'], lp['ln2_b'])


def _encoder(x, c, weights):
    for i in range(DEPTH):
        lp = {name: arr[i] for name, arr in weights.items()}
        x = _layer(x, c, lp, i)
    return x


def setup_inputs(seed: int = 0) -> dict:
    key = jax.random.key(seed)
    ks = jax.random.split(key, 32)
    f32 = jnp.float32

    def nrm(k, shape, scale):
        return jax.random.normal(k, shape, f32) * scale

    L, D, E, F = DEPTH, D_MODEL, N_EXPERTS, D_FF
    return {
        "x_prompt": nrm(ks[0], (BATCH, SEQ, D), 1.0),
        "x_sample": nrm(ks[1], (DEC_BATCH, DEC_SEQ, D), 1.0),
        "c_prompt": nrm(ks[2], (BATCH, D), 1.0),
        "c_sample": nrm(ks[3], (DEC_BATCH, D), 1.0),
        "w_ada": nrm(ks[4], (L, D, 6 * D), 0.5 * D ** -0.5),
        "b_ada": nrm(ks[5], (L, 6 * D), 0.01),
        "w_in": nrm(ks[6], (L, D, IN_WIDTH), D ** -0.5),
        "diff_lambda_q1": nrm(ks[7], (L, DIFF_HEAD_DIM), 0.1),
        "diff_lambda_k1": nrm(ks[8], (L, DIFF_HEAD_DIM), 0.1),
        "diff_lambda_q2": nrm(ks[9], (L, DIFF_HEAD_DIM), 0.1),
        "diff_lambda_k2": nrm(ks[10], (L, DIFF_HEAD_DIM), 0.1),
        "diff_subln_g": 1.0 + nrm(ks[11], (L, 2 * DIFF_HEAD_DIM), 0.01),
        "mla_q_norm_g": 1.0 + nrm(ks[12], (L, Q_LORA), 0.01),
        "mla_w_q_up": nrm(ks[13], (L, Q_LORA, MLA_HEADS * (MLA_NOPE + MLA_ROPE)), Q_LORA ** -0.5),
        "mla_kv_norm_g": 1.0 + nrm(ks[14], (L, KV_LORA), 0.01),
        "mla_w_kv_up": nrm(ks[15], (L, KV_LORA, MLA_HEADS * (MLA_NOPE + MLA_V)), KV_LORA ** -0.5),
        "w_out": nrm(ks[16], (L, MIX_WIDTH, D), DEEPNORM_BETA * MIX_WIDTH ** -0.5),
        "ln1_g": 1.0 + nrm(ks[17], (L, D), 0.01),
        "ln1_b": nrm(ks[18], (L, D), 0.01),
        "w_router": nrm(ks[19], (L, D, E), D ** -0.5),
        "b_router": nrm(ks[20], (L, E), 0.01),
        "w_mlp1": nrm(ks[21], (L, E, D, 2 * F), D ** -0.5),
        "b_mlp1": nrm(ks[22], (L, E, 2 * F), 0.01),
        "w_mlp2": nrm(ks[23], (L, E, F, D), DEEPNORM_BETA * F ** -0.5),
        "b_mlp2": nrm(ks[24], (L, E, D), 0.01),
        "ln2_g": 1.0 + nrm(ks[25], (L, D), 0.01),
        "ln2_b": nrm(ks[26], (L, D), 0.01),
    }


def reference(x_prompt, x_sample, c_prompt, c_sample, w_ada, b_ada, w_in,
              diff_lambda_q1, diff_lambda_k1, diff_lambda_q2, diff_lambda_k2, diff_subln_g,
              mla_q_norm_g, mla_w_q_up, mla_kv_norm_g, mla_w_kv_up, w_out, ln1_g, ln1_b,
              w_router, b_router, w_mlp1, b_mlp1, w_mlp2, b_mlp2, ln2_g, ln2_b):
    weights = {
        'w_ada': w_ada, 'b_ada': b_ada, 'w_in': w_in,
        'diff_lambda_q1': diff_lambda_q1, 'diff_lambda_k1': diff_lambda_k1,
        'diff_lambda_q2': diff_lambda_q2, 'diff_lambda_k2': diff_lambda_k2,
        'diff_subln_g': diff_subln_g,
        'mla_q_norm_g': mla_q_norm_g, 'mla_w_q_up': mla_w_q_up,
        'mla_kv_norm_g': mla_kv_norm_g, 'mla_w_kv_up': mla_w_kv_up,
        'w_out': w_out, 'ln1_g': ln1_g, 'ln1_b': ln1_b,
        'w_router': w_router, 'b_router': b_router,
        'w_mlp1': w_mlp1, 'b_mlp1': b_mlp1, 'w_mlp2': w_mlp2, 'b_mlp2': b_mlp2,
        'ln2_g': ln2_g, 'ln2_b': ln2_b,
    }
    y_prompt = _encoder(x_prompt, c_prompt, weights)
    y_sample = _encoder(x_sample, c_sample, weights)
    return (y_prompt, y_sample)
```

```python
import functools
import math

import jax
import jax.numpy as jnp
from jax import lax
from jax.experimental import pallas as pl
from jax.experimental.pallas import tpu as pltpu

F32 = jnp.float32
BF16 = jnp.bfloat16
I32 = jnp.int32

DIFF_HEAD_DIM = 128
MLA_NOPE = 128
MLA_ROPE = 64
MLA_V = 128
ROPE_THETA = 10000.0
TOP_K = 4
SWIGLU_LIMIT = 7.0
SWIGLU_ALPHA = 1.702
LN_EPS = 1e-5
RMS_EPS = 1e-6

LANES = 128
MLA_QK_PAD = 2 * LANES
ROPE_HALF = MLA_ROPE // 2
VMEM_LIMIT = 56 * 1024 * 1024


def _cparams(sem, vmem=VMEM_LIMIT):
    return pltpu.CompilerParams(dimension_semantics=sem, vmem_limit_bytes=vmem)


def _ln(x):
    mu = jnp.mean(x, axis=-1, keepdims=True)
    xc = x - mu
    var = jnp.mean(xc * xc, axis=-1, keepdims=True)
    return xc * lax.rsqrt(var + LN_EPS)


def _rms(x, g):
    return x * lax.rsqrt(jnp.mean(x * x, axis=-1, keepdims=True) + RMS_EPS) * g


def _dot(a, b):
    return jnp.dot(a, b, preferred_element_type=F32)


def _dot_nt(a, b):
    return lax.dot_general(a, b, (((1,), (1,)), ((), ())), preferred_element_type=F32)


def _softmax_rows(s):
    m = jnp.max(s, axis=-1, keepdims=True)
    e = jnp.exp(s - m)
    return e * (1.0 / jnp.sum(e, axis=-1, keepdims=True))


def _ada_kernel(c_ref, w_ref, b_ref, o_ref):
    c = c_ref[...]
    s = c * jax.nn.sigmoid(c)
    o_ref[...] = _dot(s.astype(BF16), w_ref[...].astype(BF16)) + b_ref[...]


def _ada(c_pad, w_ada, b_ada, tn=1024):
    bp, d = c_pad.shape
    n = w_ada.shape[1]
    tn = min(tn, n)
    return pl.pallas_call(
        _ada_kernel,
        grid=(n // tn,),
        in_specs=[pl.BlockSpec((bp, d), lambda j: (0, 0)),
                  pl.BlockSpec((d, tn), lambda j: (0, j)),
                  pl.BlockSpec((1, tn), lambda j: (0, j))],
        out_specs=pl.BlockSpec((bp, tn), lambda j: (0, j)),
        out_shape=jax.ShapeDtypeStruct((bp, n), F32),
        compiler_params=_cparams(("arbitrary",)),
        name="ada",
    )(c_pad, w_ada, b_ada.reshape(1, n))


def _rope_pad(x, c, sa, sb):
    return (x * c + pltpu.roll(x, ROPE_HALF, 1) * sa
            + pltpu.roll(x, MLA_QK_PAD - ROPE_HALF, 1) * sb)


def _ln_proj_kernel(nbp, d, dw, q_lora, kv_lora, hm,
                    xp_ref, xs_ref, mod_ref, win_ref, gq_ref, wq_ref, gkv_ref, wk_ref, wv_ref,
                    c_ref, sa_ref, sb_ref,
                    qkv_ref, qm_ref, km_ref, vm_ref):
    b = pl.program_id(0)
    x = jnp.where(b < nbp, xp_ref[0], xs_ref[0])
    mod = mod_ref[0]
    h = _ln(x) * (1.0 + mod[:, d:2 * d]) + mod[:, 0:d]
    hb = h.astype(BF16)
    o3 = 3 * dw
    qkv_ref[...] = _dot(hb, win_ref[:, 0:o3]).astype(BF16)
    cq = _dot(hb, win_ref[:, o3:o3 + q_lora])
    ckv = _dot(hb, win_ref[:, o3 + q_lora:o3 + q_lora + kv_lora])
    kpe = _dot(hb, win_ref[:, o3 + q_lora + kv_lora:o3 + q_lora + kv_lora + MLA_QK_PAD])
    c, sa, sb = c_ref[...], sa_ref[...], sb_ref[...]
    q = _dot(_rms(cq, gq_ref[...]).astype(BF16), wq_ref[...])
    ckvn = _rms(ckv, gkv_ref[...]).astype(BF16)
    kk = _dot(ckvn, wk_ref[...])
    vm_ref[...] = _dot(ckvn, wv_ref[...]).astype(BF16)
    kp = _rope_pad(kpe, c, sa, sb)
    for hh in range(hm):
        sl = slice(hh * MLA_QK_PAD, (hh + 1) * MLA_QK_PAD)
        qm_ref[:, sl] = _rope_pad(q[:, sl], c, sa, sb).astype(BF16)
        km_ref[:, sl] = (kk[:, sl] + kp).astype(BF16)


def _ln_proj(xp, xs, mod3, win, gq, wq, gkv, wk, wv, rc, rsa, rsb, tm):
    nbp, s, d = xp.shape
    nbs = xs.shape[0]
    nb = nbp + nbs
    dw = d // 2
    q_lora, kv_lora = gq.shape[1], gkv.shape[1]
    hm = wv.shape[1] // MLA_V
    nst = s // tm
    t = nb * s
    const = lambda bb, ss: (0, 0)
    row = lambda bb, ss: (bb * nst + ss, 0)
    once = dict(pipeline_mode=pl.Buffered(1))
    kern = functools.partial(_ln_proj_kernel, nbp, d, dw, q_lora, kv_lora, hm)
    return pl.pallas_call(
        kern,
        grid=(nb, nst),
        in_specs=[
            pl.BlockSpec((1, tm, d), lambda bb, ss: (jnp.minimum(bb, nbp - 1), ss, 0)),
            pl.BlockSpec((1, tm, d), lambda bb, ss: (jnp.maximum(bb - nbp, 0), ss, 0)),
            pl.BlockSpec((1, 1, 6 * d), lambda bb, ss: (bb, 0, 0)),
            pl.BlockSpec(win.shape, const, **once),
            pl.BlockSpec(gq.shape, const),
            pl.BlockSpec(wq.shape, const, **once),
            pl.BlockSpec(gkv.shape, const),
            pl.BlockSpec(wk.shape, const, **once),
            pl.BlockSpec(wv.shape, const, **once),
            pl.BlockSpec((tm, MLA_QK_PAD), lambda bb, ss: (ss, 0)),
            pl.BlockSpec((tm, MLA_QK_PAD), lambda bb, ss: (ss, 0)),
            pl.BlockSpec((tm, MLA_QK_PAD), lambda bb, ss: (ss, 0)),
        ],
        out_specs=[
            pl.BlockSpec((tm, 3 * dw), row),
            pl.BlockSpec((tm, hm * MLA_QK_PAD), row),
            pl.BlockSpec((tm, hm * MLA_QK_PAD), row),
            pl.BlockSpec((tm, hm * MLA_V), row),
        ],
        out_shape=[
            jax.ShapeDtypeStruct((t, 3 * dw), BF16),
            jax.ShapeDtypeStruct((t, hm * MLA_QK_PAD), BF16),
            jax.ShapeDtypeStruct((t, hm * MLA_QK_PAD), BF16),
            jax.ShapeDtypeStruct((t, hm * MLA_V), BF16),
        ],
        compiler_params=_cparams(("arbitrary", "arbitrary")),
        name="ln_proj",
    )(xp, xs, mod3, win, gq, wq, gkv, wk, wv, rc, rsa, rsb)


def _diff_kernel(hd, tq, s, lam_init,
                 q_ref, k_ref, v_ref, lq1_ref, lk1_ref, lq2_ref, lk2_ref, g_ref, o_ref):
    lam = (jnp.exp(jnp.sum(lq1_ref[...] * lk1_ref[...], axis=-1, keepdims=True))
           - jnp.exp(jnp.sum(lq2_ref[...] * lk2_ref[...], axis=-1, keepdims=True)) + lam_init)
    q0 = pl.program_id(1) * tq
    rowp = lax.broadcasted_iota(I32, (tq, s), 0) + q0
    colp = lax.broadcasted_iota(I32, (tq, s), 1)
    dist = jnp.abs(rowp - colp).astype(F32)
    scale = DIFF_HEAD_DIM ** -0.5
    dv = 2 * DIFF_HEAD_DIM
    g = g_ref[...]
    for h in range(hd):
        slope = 2.0 ** (-8.0 * (h + 1) / hd)
        bias = dist * slope
        p = []
        for m in range(2):
            sl = slice(h * dv + m * DIFF_HEAD_DIM, h * dv + (m + 1) * DIFF_HEAD_DIM)
            sc = _dot_nt(q_ref[:, sl], k_ref[:, sl]) * scale - bias
            p.append(_softmax_rows(sc))
        a = (p[0] - lam * p[1]).astype(BF16)
        o = _dot(a, v_ref[:, h * dv:(h + 1) * dv])
        o_ref[:, h * dv:(h + 1) * dv] = (_rms(o, g) * (1.0 - lam_init)).astype(BF16)


def _diff_attn(qkv, lq1, lk1, lq2, lk2, g, nb, s, dw, lam_init, tq):
    hd = dw // (2 * DIFF_HEAD_DIM)
    nq = s // tq
    vec = pl.BlockSpec((1, DIFF_HEAD_DIM), lambda b, i: (0, 0))
    kern = functools.partial(_diff_kernel, hd, tq, s, lam_init)
    return pl.pallas_call(
        kern,
        grid=(nb, nq),
        in_specs=[pl.BlockSpec((tq, dw), lambda b, i: (b * nq + i, 0)),
                  pl.BlockSpec((s, dw), lambda b, i: (b, 1)),
                  pl.BlockSpec((s, dw), lambda b, i: (b, 2)),
                  vec, vec, vec, vec,
                  pl.BlockSpec((1, 2 * DIFF_HEAD_DIM), lambda b, i: (0, 0))],
        out_specs=pl.BlockSpec((tq, dw), lambda b, i: (b * nq + i, 0)),
        out_shape=jax.ShapeDtypeStruct((nb * s, dw), BF16),
        compiler_params=_cparams(("arbitrary", "arbitrary")),
        name="diff_attn",
    )(qkv, qkv, qkv, lq1, lk1, lq2, lk2, g)


def _mla_kernel(hm, q_ref, k_ref, v_ref, o_ref):
    scale = (MLA_NOPE + MLA_ROPE) ** -0.5
    for h in range(hm):
        sl = slice(h * MLA_QK_PAD, (h + 1) * MLA_QK_PAD)
        p = _softmax_rows(_dot_nt(q_ref[:, sl], k_ref[:, sl]) * scale)
        o_ref[:, h * MLA_V:(h + 1) * MLA_V] = _dot(
            p.astype(BF16), v_ref[:, h * MLA_V:(h + 1) * MLA_V]).astype(BF16)


def _mla_attn(qm, km, vm, nb, s, tq):
    hm = vm.shape[1] // MLA_V
    nq = s // tq
    return pl.pallas_call(
        functools.partial(_mla_kernel, hm),
        grid=(nb, nq),
        in_specs=[pl.BlockSpec((tq, hm * MLA_QK_PAD), lambda b, i: (b * nq + i, 0)),
                  pl.BlockSpec((s, hm * MLA_QK_PAD), lambda b, i: (b, 0)),
                  pl.BlockSpec((s, hm * MLA_V), lambda b, i: (b, 0))],
        out_specs=pl.BlockSpec((tq, hm * MLA_V), lambda b, i: (b * nq + i, 0)),
        out_shape=jax.ShapeDtypeStruct((nb * s, hm * MLA_V), BF16),
        compiler_params=_cparams(("arbitrary", "arbitrary")),
        name="mla_attn",
    )(qm, km, vm)


def _post_kernel(nbp, d, dw, ne, tm, alpha,
                 od_ref, om_ref, xp_ref, xs_ref, mod_ref, wo_ref, g1_ref, b1_ref, wr_ref, br_ref,
                 x1_ref, h2_ref, idx_ref, gate_ref, rank_ref, cnt_ref, run_ref):
    b = pl.program_id(0)
    first = jnp.logical_and(b == 0, pl.program_id(1) == 0)

    @pl.when(first)
    def _():
        run_ref[...] = jnp.zeros_like(run_ref)

    mix = _dot(od_ref[...], wo_ref[0:dw, :]) + _dot(om_ref[...], wo_ref[dw:2 * dw, :])
    x = jnp.where(b < nbp, xp_ref[0], xs_ref[0])
    mod = mod_ref[0]
    x1 = _ln(alpha * x + mod[:, 2 * d:3 * d] * mix) * g1_ref[...] + b1_ref[...]
    x1_ref[...] = x1
    h2 = _ln(x1) * (1.0 + mod[:, 4 * d:5 * d]) + mod[:, 3 * d:4 * d]
    h2_ref[...] = h2
    logits = _dot(h2.astype(BF16), wr_ref[...]) + br_ref[...]

    lane = lax.broadcasted_iota(I32, (tm, ne), 1).astype(F32)
    work = logits
    vals, idxs = [], []
    for _ in range(TOP_K):
        m = jnp.max(work, axis=-1, keepdims=True)
        i = jnp.min(jnp.where(work == m, lane, float(ne)), axis=-1, keepdims=True)
        vals.append(m)
        idxs.append(i)
        work = jnp.where(lane == i, -jnp.inf, work)
    es = [jnp.exp(v - vals[0]) for v in vals]
    inv = 1.0 / (es[0] + es[1] + es[2] + es[3])

    ohs = [jnp.where(lane == i, 1.0, 0.0) for i in idxs]
    oh = ohs[0] + ohs[1] + ohs[2] + ohs[3]
    tri = jnp.where(lax.broadcasted_iota(I32, (tm, tm), 0) > lax.broadcasted_iota(I32, (tm, tm), 1),
                    1.0, 0.0)
    before = _dot(tri.astype(BF16), oh.astype(BF16)) + run_ref[...]
    run_ref[...] = run_ref[...] + jnp.sum(oh, axis=0, keepdims=True)
    cnt_ref[...] = run_ref[...]

    lane_o = lax.broadcasted_iota(I32, (tm, LANES), 1)
    idx_o = jnp.zeros((tm, LANES), I32)
    gate_o = jnp.zeros((tm, LANES), F32)
    rank_o = jnp.zeros((tm, LANES), I32)
    for k in range(TOP_K):
        rk = jnp.sum(ohs[k] * before, axis=-1, keepdims=True).astype(I32)
        idx_o = jnp.where(lane_o == k, idxs[k].astype(I32), idx_o)
        gate_o = jnp.where(lane_o == k, es[k] * inv, gate_o)
        rank_o = jnp.where(lane_o == k, rk, rank_o)
    idx_ref[...] = idx_o
    gate_ref[...] = gate_o
    rank_ref[...] = rank_o


def _post(od, om, xp, xs, mod3, wo, g1, b1, wr, br, alpha, tm):
    nbp, s, d = xp.shape
    nb = nbp + xs.shape[0]
    dw = d // 2
    ne = wr.shape[1]
    nst = s // tm
    t = nb * s
    const = lambda bb, ss: (0, 0)
    row = lambda bb, ss: (bb * nst + ss, 0)
    kern = functools.partial(_post_kernel, nbp, d, dw, ne, tm, alpha)
    return pl.pallas_call(
        kern,
        grid=(nb, nst),
        in_specs=[
            pl.BlockSpec((tm, dw), row),
            pl.BlockSpec((tm, dw), row),
            pl.BlockSpec((1, tm, d), lambda bb, ss: (jnp.minimum(bb, nbp - 1), ss, 0)),
            pl.BlockSpec((1, tm, d), lambda bb, ss: (jnp.maximum(bb - nbp, 0), ss, 0)),
            pl.BlockSpec((1, 1, 6 * d), lambda bb, ss: (bb, 0, 0)),
            pl.BlockSpec(wo.shape, const, pipeline_mode=pl.Buffered(1)),
            pl.BlockSpec((1, d), const),
            pl.BlockSpec((1, d), const),
            pl.BlockSpec(wr.shape, const),
            pl.BlockSpec((1, ne), const),
        ],
        out_specs=[
            pl.BlockSpec((tm, d), row),
            pl.BlockSpec((tm, d), row),
            pl.BlockSpec((tm, LANES), row),
            pl.BlockSpec((tm, LANES), row),
            pl.BlockSpec((tm, LANES), row),
            pl.BlockSpec((1, ne), const),
        ],
        out_shape=[
            jax.ShapeDtypeStruct((t, d), F32),
            jax.ShapeDtypeStruct((t, d), F32),
            jax.ShapeDtypeStruct((t, LANES), I32),
            jax.ShapeDtypeStruct((t, LANES), F32),
            jax.ShapeDtypeStruct((t, LANES), I32),
            jax.ShapeDtypeStruct((1, ne), F32),
        ],
        scratch_shapes=[pltpu.VMEM((1, ne), F32)],
        compiler_params=_cparams(("arbitrary", "arbitrary")),
        name="post_mixer",
    )(od, om, xp, xs, mod3, wo, g1, b1, wr, br)


def _gather_kernel(tg, tok_ref, src_ref, out_ref, sem):
    def issue(r, carry):
        pltpu.make_async_copy(src_ref.at[pl.ds(tok_ref[r], 1)], out_ref.at[pl.ds(r, 1)], sem).start()
        return carry

    lax.fori_loop(0, tg, issue, 0)
    pltpu.make_async_copy(src_ref.at[pl.ds(0, tg)], out_ref, sem).wait()


def _gather_rows(tok, src, tg):
    p = tok.shape[0]
    d = src.shape[1]
    return pl.pallas_call(
        functools.partial(_gather_kernel, tg),
        grid=(p // tg,),
        in_specs=[pl.BlockSpec((tg,), lambda i: (i,), memory_space=pltpu.SMEM),
                  pl.BlockSpec(memory_space=pl.ANY)],
        out_specs=pl.BlockSpec((tg, d), lambda i: (i, 0)),
        out_shape=jax.ShapeDtypeStruct((p, d), src.dtype),
        scratch_shapes=[pltpu.SemaphoreType.DMA(())],
        compiler_params=_cparams(("arbitrary",)),
        name="gather_rows",
    )(tok, src)


def _expert_kernel(be_ref, nu_ref, x_ref, wg_ref, wu_ref, bg_ref, bu_ref, w2_ref, b2_ref,
                   o_ref, xb_ref):
    i = pl.program_id(0)
    j = pl.program_id(1)
    used = i < nu_ref[0]

    @pl.when(j == 0)
    def _():
        xb_ref[...] = x_ref[...].astype(BF16)
        o_ref[...] = jnp.broadcast_to(b2_ref[0], o_ref.shape)

    @pl.when(used)
    def _():
        xb = xb_ref[...]
        hg = _dot(xb, wg_ref[0]) + bg_ref[0]
        hu = _dot(xb, wu_ref[0]) + bu_ref[0]
        glu = jnp.minimum(hg, SWIGLU_LIMIT)
        lin = jnp.clip(hu, -SWIGLU_LIMIT, SWIGLU_LIMIT)
        act = glu * jax.nn.sigmoid(SWIGLU_ALPHA * glu) * (lin + 1.0)
        o_ref[...] += _dot(act.astype(BF16), w2_ref[0])


def _experts(block_e, n_used, xs, wg, wu, bg, bu, w2, b2, tm, tf):
    p, d = xs.shape
    ne, _, f = wg.shape
    nblk = p // tm
    nf = f // tf

    def wcol(i, j, be, nu):
        return (be[i], 0, jnp.where(i < nu[0], j, nf - 1))

    def wrow(i, j, be, nu):
        return (be[i], jnp.where(i < nu[0], j, nf - 1), 0)

    grid_spec = pltpu.PrefetchScalarGridSpec(
        num_scalar_prefetch=2,
        grid=(nblk, nf),
        in_specs=[
            pl.BlockSpec((tm, d), lambda i, j, be, nu: (i, 0)),
            pl.BlockSpec((1, d, tf), wcol),
            pl.BlockSpec((1, d, tf), wcol),
            pl.BlockSpec((1, 1, tf), wcol),
            pl.BlockSpec((1, 1, tf), wcol),
            pl.BlockSpec((1, tf, d), wrow),
            pl.BlockSpec((1, 1, d), lambda i, j, be, nu: (be[i], 0, 0)),
        ],
        out_specs=pl.BlockSpec((tm, d), lambda i, j, be, nu: (i, 0)),
        scratch_shapes=[pltpu.VMEM((tm, d), BF16)],
    )
    return pl.pallas_call(
        _expert_kernel,
        grid_spec=grid_spec,
        out_shape=jax.ShapeDtypeStruct((p, d), F32),
        compiler_params=_cparams(("arbitrary", "arbitrary")),
        name="experts",
    )(block_e, n_used, xs, wg, wu, bg, bu, w2, b2)


def _final_kernel(d, tm, alpha,
                  dest_ref, x1_ref, mod_ref, gate_ref, src_ref, g2_ref, b2_ref, o_ref, rows_ref, sem):
    def issue(r, carry):
        for k in range(TOP_K):
            pltpu.make_async_copy(src_ref.at[pl.ds(dest_ref[r * TOP_K + k], 1)],
                                  rows_ref.at[k, pl.ds(r, 1)], sem).start()
        return carry

    lax.fori_loop(0, tm, issue, 0)
    for k in range(TOP_K):
        pltpu.make_async_copy(src_ref.at[pl.ds(0, tm)], rows_ref.at[k], sem).wait()
    gate = gate_ref[...]
    y = gate[:, 0:1] * rows_ref[0]
    for k in range(1, TOP_K):
        y = y + gate[:, k:k + 1] * rows_ref[k]
    mod = mod_ref[0]
    z = alpha * x1_ref[...] + mod[:, 5 * d:6 * d] * y
    o_ref[0] = _ln(z) * g2_ref[...] + b2_ref[...]


def _final(dest_flat, x1, mod3, gate, eo, g2, b2, alpha, b0, nbg, s, tm):
    d = x1.shape[1]
    nst = s // tm
    off = b0 * nst
    kern = functools.partial(_final_kernel, d, tm, alpha)
    return pl.pallas_call(
        kern,
        grid=(nbg * nst,),
        in_specs=[
            pl.BlockSpec((tm * TOP_K,), lambda i: (i + off,), memory_space=pltpu.SMEM),
            pl.BlockSpec((tm, d), lambda i: (i + off, 0)),
            pl.BlockSpec((1, 1, 6 * d), lambda i: (b0 + i // nst, 0, 0)),
            pl.BlockSpec((tm, LANES), lambda i: (i + off, 0)),
            pl.BlockSpec(memory_space=pl.ANY),
            pl.BlockSpec((1, d), lambda i: (0, 0)),
            pl.BlockSpec((1, d), lambda i: (0, 0)),
        ],
        out_specs=pl.BlockSpec((1, tm, d), lambda i: (i // nst, i % nst, 0)),
        out_shape=jax.ShapeDtypeStruct((nbg, s, d), F32),
        scratch_shapes=[pltpu.VMEM((TOP_K, tm, d), F32), pltpu.SemaphoreType.DMA(())],
        compiler_params=_cparams(("arbitrary",)),
        name="final_combine",
    )(dest_flat, x1, mod3, gate, eo, g2, b2)


def _rope_tables(s):
    inv = 1.0 / (ROPE_THETA ** (jnp.arange(0, MLA_ROPE, 2, dtype=F32) / MLA_ROPE))
    ang = jnp.arange(s, dtype=F32)[:, None] * inv[None, :]
    cos, sin = jnp.cos(ang), jnp.sin(ang)
    one = jnp.ones((s, MLA_NOPE), F32)
    z32 = jnp.zeros((s, ROPE_HALF), F32)
    ztail = jnp.zeros((s, MLA_QK_PAD - MLA_NOPE - MLA_ROPE), F32)
    znope = jnp.zeros((s, MLA_NOPE), F32)
    c = jnp.concatenate([one, cos, cos, ztail], axis=1)
    sa = jnp.concatenate([znope, z32, sin, ztail], axis=1)
    sb = jnp.concatenate([znope, -sin, z32, ztail], axis=1)
    return c, sa, sb


def _pad_cols(w, width):
    return jnp.pad(w, ((0, 0), (0, width - w.shape[1])))


def _layer(xp, xs, mod3, lp, layer_idx, alpha):
    nbp, s, d = xp.shape
    nb = nbp + xs.shape[0]
    dw = d // 2
    hm = (d - dw) // MLA_V
    ne = lp["w_router"].shape[1]
    f = lp["w_mlp2"].shape[1]
    t = nb * s
    tm = min(256, s)
    tq = min(256, s)

    w_in = lp["w_in"]
    q_lora = lp["mla_q_norm_g"].shape[0]
    kv_lora = lp["mla_kv_norm_g"].shape[0]
    o3, o4, o5 = 3 * dw, 3 * dw + q_lora, 3 * dw + q_lora + kv_lora
    kpe_w = jnp.concatenate([jnp.zeros((d, MLA_NOPE), F32), w_in[:, o5:],
                             jnp.zeros((d, MLA_QK_PAD - MLA_NOPE - MLA_ROPE), F32)], axis=1)
    win = jnp.concatenate([w_in[:, :o5], kpe_w], axis=1).astype(BF16)
    wq = lp["mla_w_q_up"].reshape(q_lora, hm, MLA_NOPE + MLA_ROPE)
    wq = jnp.pad(wq, ((0, 0), (0, 0), (0, MLA_QK_PAD - MLA_NOPE - MLA_ROPE)))
    wq = wq.reshape(q_lora, hm * MLA_QK_PAD).astype(BF16)
    wkv = lp["mla_w_kv_up"].reshape(kv_lora, hm, MLA_NOPE + MLA_V)
    wk = jnp.pad(wkv[:, :, :MLA_NOPE], ((0, 0), (0, 0), (0, MLA_QK_PAD - MLA_NOPE)))
    wk = wk.reshape(kv_lora, hm * MLA_QK_PAD).astype(BF16)
    wv = wkv[:, :, MLA_NOPE:].reshape(kv_lora, hm * MLA_V).astype(BF16)
    rc, rsa, rsb = _rope_tables(s)

    qkv, qm, km, vm = _ln_proj(xp, xs, mod3, win, lp["mla_q_norm_g"].reshape(1, -1), wq,
                               lp["mla_kv_norm_g"].reshape(1, -1), wk, wv, rc, rsa, rsb, tm)

    lam_init = 0.8 - 0.6 * math.exp(-0.3 * layer_idx)
    od = _diff_attn(qkv, lp["diff_lambda_q1"].reshape(1, -1), lp["diff_lambda_k1"].reshape(1, -1),
                    lp["diff_lambda_q2"].reshape(1, -1), lp["diff_lambda_k2"].reshape(1, -1),
                    lp["diff_subln_g"].reshape(1, -1), nb, s, dw, lam_init, tq)
    om = _mla_attn(qm, km, vm, nb, s, tq)

    x1, h2, idx, gate, rank, cnt = _post(
        od, om, xp, xs, mod3, lp["w_out"].astype(BF16), lp["ln1_g"].reshape(1, -1),
        lp["ln1_b"].reshape(1, -1), lp["w_router"].astype(BF16), lp["b_router"].reshape(1, -1),
        alpha, tm)

    te = min(512, t * TOP_K // ne)
    nblk = (t * TOP_K) // te + ne
    p = nblk * te
    counts = cnt[0].astype(I32)
    padded = (counts + te - 1) // te * te
    pends = jnp.cumsum(padded)
    pstarts = pends - padded
    idx4 = idx[:, :TOP_K]
    onehot = idx4[:, :, None] == jnp.arange(ne, dtype=I32)[None, None, :]
    dest = jnp.sum(jnp.where(onehot, pstarts[None, None, :], 0), axis=-1) + rank[:, :TOP_K]
    dest_flat = dest.reshape(-1).astype(I32)
    buf_tok = jnp.zeros((p,), I32).at[dest_flat].set(jnp.arange(t * TOP_K, dtype=I32) // TOP_K)
    n_used = (pends[-1] // te).astype(I32)
    blk = jnp.minimum(jnp.arange(nblk, dtype=I32), n_used - 1)
    block_e = jnp.minimum(jnp.searchsorted(pends, blk * te, side="right"), ne - 1).astype(I32)

    xsorted = _gather_rows(buf_tok, h2, min(256, te))

    w1 = lp["w_mlp1"]
    b1 = lp["b_mlp1"]
    wg = w1[:, :, 0::2].astype(BF16)
    wu = w1[:, :, 1::2].astype(BF16)
    bg = b1[:, 0::2].reshape(ne, 1, f)
    bu = b1[:, 1::2].reshape(ne, 1, f)
    eo = _experts(block_e, n_used.reshape(1), xsorted, wg, wu, bg, bu, lp["w_mlp2"].astype(BF16),
                  lp["b_mlp2"].reshape(ne, 1, d), te, min(512, f))

    g2 = lp["ln2_g"].reshape(1, -1)
    b2 = lp["ln2_b"].reshape(1, -1)
    yp = _final(dest_flat, x1, mod3, gate, eo, g2, b2, alpha, 0, nbp, s, tm)
    ys = _final(dest_flat, x1, mod3, gate, eo, g2, b2, alpha, nbp, nb - nbp, s, tm)
    return yp, ys


def kernel(x_prompt, x_sample, c_prompt, c_sample, w_ada, b_ada, w_in, diff_lambda_q1, diff_lambda_k1, diff_lambda_q2, diff_lambda_k2, diff_subln_g, mla_q_norm_g, mla_w_q_up, mla_kv_norm_g, mla_w_kv_up, w_out, ln1_g, ln1_b, w_router, b_router, w_mlp1, b_mlp1, w_mlp2, b_mlp2, ln2_g, ln2_b):
    weights = dict(
        w_ada=w_ada, b_ada=b_ada, w_in=w_in, diff_lambda_q1=diff_lambda_q1,
        diff_lambda_k1=diff_lambda_k1, diff_lambda_q2=diff_lambda_q2, diff_lambda_k2=diff_lambda_k2,
        diff_subln_g=diff_subln_g, mla_q_norm_g=mla_q_norm_g, mla_w_q_up=mla_w_q_up,
        mla_kv_norm_g=mla_kv_norm_g, mla_w_kv_up=mla_w_kv_up, w_out=w_out, ln1_g=ln1_g, ln1_b=ln1_b,
        w_router=w_router, b_router=b_router, w_mlp1=w_mlp1, b_mlp1=b_mlp1, w_mlp2=w_mlp2,
        b_mlp2=b_mlp2, ln2_g=ln2_g, ln2_b=ln2_b)
    depth = w_ada.shape[0]
    assert x_prompt.shape[1:] == x_sample.shape[1:], "the two request groups are batched together"
    nbp, nbs = x_prompt.shape[0], x_sample.shape[0]
    nb = nbp + nbs
    d = x_prompt.shape[2]
    alpha = (2.0 * depth) ** 0.25
    bpad = -(-nb // 8) * 8
    c_pad = jnp.pad(jnp.concatenate([c_prompt, c_sample], axis=0), ((0, bpad - nb), (0, 0)))
    xp, xs = x_prompt, x_sample
    for i in range(depth):
        lp = {name: arr[i] for name, arr in weights.items()}
        mod3 = _ada(c_pad, lp["w_ada"], lp["b_ada"]).reshape(bpad, 1, 6 * d)
        xp, xs = _layer(xp, xs, mod3, lp, i, alpha)
    return (xp, xs)
```

```python
import functools
import math

import jax
import jax.numpy as jnp
from jax import lax
from jax.experimental import pallas as pl
from jax.experimental.pallas import tpu as pltpu

F32 = jnp.float32
BF16 = jnp.bfloat16
I32 = jnp.int32

DIFF_HEAD_DIM = 128
MLA_NOPE = 128
MLA_ROPE = 64
MLA_V = 128
ROPE_THETA = 10000.0
TOP_K = 4
SWIGLU_LIMIT = 7.0
SWIGLU_ALPHA = 1.702
LN_EPS = 1e-5
RMS_EPS = 1e-6

LANES = 128
MLA_QK_PAD = 2 * LANES
ROPE_HALF = MLA_ROPE // 2
VMEM_LIMIT = 56 * 1024 * 1024


def _cparams(sem, vmem=VMEM_LIMIT):
    return pltpu.CompilerParams(dimension_semantics=sem, vmem_limit_bytes=vmem)


def _ln(x):
    mu = jnp.mean(x, axis=-1, keepdims=True)
    xc = x - mu
    var = jnp.mean(xc * xc, axis=-1, keepdims=True)
    return xc * lax.rsqrt(var + LN_EPS)


def _rms(x, g):
    return x * lax.rsqrt(jnp.mean(x * x, axis=-1, keepdims=True) + RMS_EPS) * g


def _dot(a, b):
    return jnp.dot(a, b, preferred_element_type=F32)


def _dot_nt(a, b):
    return lax.dot_general(a, b, (((1,), (1,)), ((), ())), preferred_element_type=F32)


def _softmax_rows(s):
    m = jnp.max(s, axis=-1, keepdims=True)
    e = jnp.exp(s - m)
    return e * (1.0 / jnp.sum(e, axis=-1, keepdims=True))


def _store_token_major(ref, x, rows, at=()):
    nch = x.shape[1] // LANES
    for c in range(nch):
        ref[at + (pl.ds(c, rows, stride=nch), slice(None))] = x[:, c * LANES:(c + 1) * LANES]


def _load_token_major(ref, c, rows, nch, at=()):
    return ref[at + (pl.ds(c, rows, stride=nch), slice(None))]


def _ada_kernel(c_ref, w_ref, b_ref, o_ref):
    c = c_ref[...]
    s = c * jax.nn.sigmoid(c)
    o_ref[...] = _dot(s.astype(BF16), w_ref[...].astype(BF16)) + b_ref[...]


def _ada(c_pad, w_ada, b_ada, tn=1024):
    bp, d = c_pad.shape
    n = w_ada.shape[1]
    tn = min(tn, n)
    return pl.pallas_call(
        _ada_kernel,
        grid=(n // tn,),
        in_specs=[pl.BlockSpec((bp, d), lambda j: (0, 0)),
                  pl.BlockSpec((d, tn), lambda j: (0, j)),
                  pl.BlockSpec((1, tn), lambda j: (0, j))],
        out_specs=pl.BlockSpec((bp, tn), lambda j: (0, j)),
        out_shape=jax.ShapeDtypeStruct((bp, n), F32),
        compiler_params=_cparams(("arbitrary",)),
        name="ada",
    )(c_pad, w_ada, b_ada.reshape(1, n))


def _rope_pad(x, c, sa, sb):
    return (x * c + pltpu.roll(x, ROPE_HALF, 1) * sa
            + pltpu.roll(x, MLA_QK_PAD - ROPE_HALF, 1) * sb)


def _ln_proj_kernel(nbp, d, dw, q_lora, kv_lora, hm,
                    xp_ref, xs_ref, mod_ref, win_ref, gq_ref, wq_ref, gkv_ref, wk_ref, wv_ref,
                    c_ref, sa_ref, sb_ref,
                    qkv_ref, qm_ref, km_ref, vm_ref):
    b = pl.program_id(0)
    x = jnp.where(b < nbp, xp_ref[0], xs_ref[0])
    mod = mod_ref[0]
    h = _ln(x) * (1.0 + mod[:, d:2 * d]) + mod[:, 0:d]
    hb = h.astype(BF16)
    o3 = 3 * dw
    qkv_ref[...] = _dot(hb, win_ref[:, 0:o3]).astype(BF16)
    cq = _dot(hb, win_ref[:, o3:o3 + q_lora])
    ckv = _dot(hb, win_ref[:, o3 + q_lora:o3 + q_lora + kv_lora])
    kpe = _dot(hb, win_ref[:, o3 + q_lora + kv_lora:o3 + q_lora + kv_lora + MLA_QK_PAD])
    c, sa, sb = c_ref[...], sa_ref[...], sb_ref[...]
    q = _dot(_rms(cq, gq_ref[...]).astype(BF16), wq_ref[...])
    ckvn = _rms(ckv, gkv_ref[...]).astype(BF16)
    kk = _dot(ckvn, wk_ref[...])
    vm_ref[...] = _dot(ckvn, wv_ref[...]).astype(BF16)
    kp = _rope_pad(kpe, c, sa, sb)
    for hh in range(hm):
        sl = slice(hh * MLA_QK_PAD, (hh + 1) * MLA_QK_PAD)
        qm_ref[:, sl] = _rope_pad(q[:, sl], c, sa, sb).astype(BF16)
        km_ref[:, sl] = (kk[:, sl] + kp).astype(BF16)


def _ln_proj(xp, xs, mod3, win, gq, wq, gkv, wk, wv, rc, rsa, rsb, tm):
    nbp, s, d = xp.shape
    nbs = xs.shape[0]
    nb = nbp + nbs
    dw = d // 2
    q_lora, kv_lora = gq.shape[1], gkv.shape[1]
    hm = wv.shape[1] // MLA_V
    nst = s // tm
    t = nb * s
    const = lambda bb, ss: (0, 0)
    row = lambda bb, ss: (bb * nst + ss, 0)
    once = dict(pipeline_mode=pl.Buffered(1))
    kern = functools.partial(_ln_proj_kernel, nbp, d, dw, q_lora, kv_lora, hm)
    return pl.pallas_call(
        kern,
        grid=(nb, nst),
        in_specs=[
            pl.BlockSpec((1, tm, d), lambda bb, ss: (jnp.minimum(bb, nbp - 1), ss, 0)),
            pl.BlockSpec((1, tm, d), lambda bb, ss: (jnp.maximum(bb - nbp, 0), ss, 0)),
            pl.BlockSpec((1, 1, 6 * d), lambda bb, ss: (bb, 0, 0)),
            pl.BlockSpec(win.shape, const, **once),
            pl.BlockSpec(gq.shape, const),
            pl.BlockSpec(wq.shape, const, **once),
            pl.BlockSpec(gkv.shape, const),
            pl.BlockSpec(wk.shape, const, **once),
            pl.BlockSpec(wv.shape, const, **once),
            pl.BlockSpec((tm, MLA_QK_PAD), lambda bb, ss: (ss, 0)),
            pl.BlockSpec((tm, MLA_QK_PAD), lambda bb, ss: (ss, 0)),
            pl.BlockSpec((tm, MLA_QK_PAD), lambda bb, ss: (ss, 0)),
        ],
        out_specs=[
            pl.BlockSpec((tm, 3 * dw), row),
            pl.BlockSpec((tm, hm * MLA_QK_PAD), row),
            pl.BlockSpec((tm, hm * MLA_QK_PAD), row),
            pl.BlockSpec((tm, hm * MLA_V), row),
        ],
        out_shape=[
            jax.ShapeDtypeStruct((t, 3 * dw), BF16),
            jax.ShapeDtypeStruct((t, hm * MLA_QK_PAD), BF16),
            jax.ShapeDtypeStruct((t, hm * MLA_QK_PAD), BF16),
            jax.ShapeDtypeStruct((t, hm * MLA_V), BF16),
        ],
        compiler_params=_cparams(("arbitrary", "arbitrary")),
        name="ln_proj",
    )(xp, xs, mod3, win, gq, wq, gkv, wk, wv, rc, rsa, rsb)


def _diff_kernel(hd, tq, s, lam_init,
                 q_ref, k_ref, v_ref, lq1_ref, lk1_ref, lq2_ref, lk2_ref, g_ref, o_ref):
    lam = (jnp.exp(jnp.sum(lq1_ref[...] * lk1_ref[...], axis=-1, keepdims=True))
           - jnp.exp(jnp.sum(lq2_ref[...] * lk2_ref[...], axis=-1, keepdims=True)) + lam_init)
    q0 = pl.program_id(1) * tq
    rowp = lax.broadcasted_iota(I32, (tq, s), 0) + q0
    colp = lax.broadcasted_iota(I32, (tq, s), 1)
    dist = jnp.abs(rowp - colp).astype(F32)
    scale = DIFF_HEAD_DIM ** -0.5
    dv = 2 * DIFF_HEAD_DIM
    g = g_ref[...]
    for h in range(hd):
        slope = 2.0 ** (-8.0 * (h + 1) / hd)
        bias = dist * slope
        p = []
        for m in range(2):
            sl = slice(h * dv + m * DIFF_HEAD_DIM, h * dv + (m + 1) * DIFF_HEAD_DIM)
            sc = _dot_nt(q_ref[:, sl], k_ref[:, sl]) * scale - bias
            p.append(_softmax_rows(sc))
        a = (p[0] - lam * p[1]).astype(BF16)
        o = _dot(a, v_ref[:, h * dv:(h + 1) * dv])
        o_ref[:, h * dv:(h + 1) * dv] = (_rms(o, g) * (1.0 - lam_init)).astype(BF16)


def _diff_attn(qkv, lq1, lk1, lq2, lk2, g, nb, s, dw, lam_init, tq):
    hd = dw // (2 * DIFF_HEAD_DIM)
    nq = s // tq
    vec = pl.BlockSpec((1, DIFF_HEAD_DIM), lambda b, i: (0, 0))
    kern = functools.partial(_diff_kernel, hd, tq, s, lam_init)
    return pl.pallas_call(
        kern,
        grid=(nb, nq),
        in_specs=[pl.BlockSpec((tq, dw), lambda b, i: (b * nq + i, 0)),
                  pl.BlockSpec((s, dw), lambda b, i: (b, 1)),
                  pl.BlockSpec((s, dw), lambda b, i: (b, 2)),
                  vec, vec, vec, vec,
                  pl.BlockSpec((1, 2 * DIFF_HEAD_DIM), lambda b, i: (0, 0))],
        out_specs=pl.BlockSpec((tq, dw), lambda b, i: (b * nq + i, 0)),
        out_shape=jax.ShapeDtypeStruct((nb * s, dw), BF16),
        compiler_params=_cparams(("arbitrary", "arbitrary")),
        name="diff_attn",
    )(qkv, qkv, qkv, lq1, lk1, lq2, lk2, g)


def _mla_kernel(hm, q_ref, k_ref, v_ref, o_ref):
    scale = (MLA_NOPE + MLA_ROPE) ** -0.5
    for h in range(hm):
        sl = slice(h * MLA_QK_PAD, (h + 1) * MLA_QK_PAD)
        p = _softmax_rows(_dot_nt(q_ref[:, sl], k_ref[:, sl]) * scale)
        o_ref[:, h * MLA_V:(h + 1) * MLA_V] = _dot(
            p.astype(BF16), v_ref[:, h * MLA_V:(h + 1) * MLA_V]).astype(BF16)


def _mla_attn(qm, km, vm, nb, s, tq):
    hm = vm.shape[1] // MLA_V
    nq = s // tq
    return pl.pallas_call(
        functools.partial(_mla_kernel, hm),
        grid=(nb, nq),
        in_specs=[pl.BlockSpec((tq, hm * MLA_QK_PAD), lambda b, i: (b * nq + i, 0)),
                  pl.BlockSpec((s, hm * MLA_QK_PAD), lambda b, i: (b, 0)),
                  pl.BlockSpec((s, hm * MLA_V), lambda b, i: (b, 0))],
        out_specs=pl.BlockSpec((tq, hm * MLA_V), lambda b, i: (b * nq + i, 0)),
        out_shape=jax.ShapeDtypeStruct((nb * s, hm * MLA_V), BF16),
        compiler_params=_cparams(("arbitrary", "arbitrary")),
        name="mla_attn",
    )(qm, km, vm)


def _post_kernel(nbp, d, dw, ne, tm, alpha,
                 od_ref, om_ref, xp_ref, xs_ref, mod_ref, wo_ref, g1_ref, b1_ref, wr_ref, br_ref,
                 x1_ref, h2_ref, idx_ref, gate_ref, rank_ref, cnt_ref, run_ref):
    b = pl.program_id(0)
    first = jnp.logical_and(b == 0, pl.program_id(1) == 0)

    @pl.when(first)
    def _():
        run_ref[...] = jnp.zeros_like(run_ref)

    mix = _dot(od_ref[...], wo_ref[0:dw, :]) + _dot(om_ref[...], wo_ref[dw:2 * dw, :])
    x = jnp.where(b < nbp, xp_ref[0], xs_ref[0])
    mod = mod_ref[0]
    x1 = _ln(alpha * x + mod[:, 2 * d:3 * d] * mix) * g1_ref[...] + b1_ref[...]
    x1_ref[...] = x1
    h2 = _ln(x1) * (1.0 + mod[:, 4 * d:5 * d]) + mod[:, 3 * d:4 * d]
    _store_token_major(h2_ref, h2, tm)
    logits = _dot(h2.astype(BF16), wr_ref[...]) + br_ref[...]

    lane = lax.broadcasted_iota(I32, (tm, ne), 1).astype(F32)
    work = logits
    vals, idxs = [], []
    for _ in range(TOP_K):
        m = jnp.max(work, axis=-1, keepdims=True)
        i = jnp.min(jnp.where(work == m, lane, float(ne)), axis=-1, keepdims=True)
        vals.append(m)
        idxs.append(i)
        work = jnp.where(lane == i, -jnp.inf, work)
    es = [jnp.exp(v - vals[0]) for v in vals]
    inv = 1.0 / (es[0] + es[1] + es[2] + es[3])

    ohs = [jnp.where(lane == i, 1.0, 0.0) for i in idxs]
    oh = ohs[0] + ohs[1] + ohs[2] + ohs[3]
    tri = jnp.where(lax.broadcasted_iota(I32, (tm, tm), 0) > lax.broadcasted_iota(I32, (tm, tm), 1),
                    1.0, 0.0)
    before = _dot(tri.astype(BF16), oh.astype(BF16)) + run_ref[...]
    run_ref[...] = run_ref[...] + jnp.sum(oh, axis=0, keepdims=True)
    cnt_ref[...] = run_ref[...]

    lane_o = lax.broadcasted_iota(I32, (tm, LANES), 1)
    idx_o = jnp.zeros((tm, LANES), I32)
    gate_o = jnp.zeros((tm, LANES), F32)
    rank_o = jnp.zeros((tm, LANES), I32)
    for k in range(TOP_K):
        rk = jnp.sum(ohs[k] * before, axis=-1, keepdims=True).astype(I32)
        idx_o = jnp.where(lane_o == k, idxs[k].astype(I32), idx_o)
        gate_o = jnp.where(lane_o == k, es[k] * inv, gate_o)
        rank_o = jnp.where(lane_o == k, rk, rank_o)
    idx_ref[...] = idx_o
    gate_ref[...] = gate_o
    rank_ref[...] = rank_o


def _post(od, om, xp, xs, mod3, wo, g1, b1, wr, br, alpha, tm):
    nbp, s, d = xp.shape
    nb = nbp + xs.shape[0]
    dw = d // 2
    ne = wr.shape[1]
    nst = s // tm
    t = nb * s
    const = lambda bb, ss: (0, 0)
    row = lambda bb, ss: (bb * nst + ss, 0)
    kern = functools.partial(_post_kernel, nbp, d, dw, ne, tm, alpha)
    return pl.pallas_call(
        kern,
        grid=(nb, nst),
        in_specs=[
            pl.BlockSpec((tm, dw), row),
            pl.BlockSpec((tm, dw), row),
            pl.BlockSpec((1, tm, d), lambda bb, ss: (jnp.minimum(bb, nbp - 1), ss, 0)),
            pl.BlockSpec((1, tm, d), lambda bb, ss: (jnp.maximum(bb - nbp, 0), ss, 0)),
            pl.BlockSpec((1, 1, 6 * d), lambda bb, ss: (bb, 0, 0)),
            pl.BlockSpec(wo.shape, const, pipeline_mode=pl.Buffered(1)),
            pl.BlockSpec((1, d), const),
            pl.BlockSpec((1, d), const),
            pl.BlockSpec(wr.shape, const),
            pl.BlockSpec((1, ne), const),
        ],
        out_specs=[
            pl.BlockSpec((tm, d), row),
            pl.BlockSpec((tm * (d // LANES), LANES), row),
            pl.BlockSpec((tm, LANES), row),
            pl.BlockSpec((tm, LANES), row),
            pl.BlockSpec((tm, LANES), row),
            pl.BlockSpec((1, ne), const),
        ],
        out_shape=[
            jax.ShapeDtypeStruct((t, d), F32),
            jax.ShapeDtypeStruct((t * (d // LANES), LANES), F32),
            jax.ShapeDtypeStruct((t, LANES), I32),
            jax.ShapeDtypeStruct((t, LANES), F32),
            jax.ShapeDtypeStruct((t, LANES), I32),
            jax.ShapeDtypeStruct((1, ne), F32),
        ],
        scratch_shapes=[pltpu.VMEM((1, ne), F32)],
        compiler_params=_cparams(("arbitrary", "arbitrary")),
        name="post_mixer",
    )(od, om, xp, xs, mod3, wo, g1, b1, wr, br)


def _gather_kernel(tg, nch, tok_ref, src_ref, out_ref, sem):
    def issue(r, carry):
        src = pl.multiple_of(tok_ref[r] * nch, nch)
        dst = pl.multiple_of(r * nch, nch)
        pltpu.make_async_copy(src_ref.at[pl.ds(src, nch)], out_ref.at[pl.ds(dst, nch)], sem).start()
        return carry

    lax.fori_loop(0, tg, issue, 0, unroll=8)
    pltpu.make_async_copy(src_ref.at[pl.ds(0, tg * nch)], out_ref, sem).wait()


def _gather_rows(tok, src, nch, tg):
    p = tok.shape[0]
    return pl.pallas_call(
        functools.partial(_gather_kernel, tg, nch),
        grid=(p // tg,),
        in_specs=[pl.BlockSpec((tg,), lambda i: (i,), memory_space=pltpu.SMEM),
                  pl.BlockSpec(memory_space=pl.ANY)],
        out_specs=pl.BlockSpec((tg * nch, LANES), lambda i: (i, 0)),
        out_shape=jax.ShapeDtypeStruct((p * nch, LANES), src.dtype),
        scratch_shapes=[pltpu.SemaphoreType.DMA(())],
        compiler_params=_cparams(("arbitrary",)),
        name="gather_rows",
    )(tok, src)


def _expert_kernel(tm, nch, nf, be_ref, nu_ref, x_ref, wg_ref, wu_ref, bg_ref, bu_ref, w2_ref, b2_ref,
                   o_ref, xb_ref, acc_ref):
    i = pl.program_id(0)
    j = pl.program_id(1)
    used = i < nu_ref[0]

    @pl.when(j == 0)
    def _():
        for c in range(nch):
            xb_ref[:, c * LANES:(c + 1) * LANES] = _load_token_major(x_ref, c, tm, nch).astype(BF16)
        acc_ref[...] = jnp.broadcast_to(b2_ref[0], acc_ref.shape)

    @pl.when(used)
    def _():
        xb = xb_ref[...]
        hg = _dot_nt(xb, wg_ref[0]) + bg_ref[0]
        hu = _dot_nt(xb, wu_ref[0]) + bu_ref[0]
        glu = jnp.minimum(hg, SWIGLU_LIMIT)
        lin = jnp.clip(hu, -SWIGLU_LIMIT, SWIGLU_LIMIT)
        act = glu * jax.nn.sigmoid(SWIGLU_ALPHA * glu) * (lin + 1.0)
        acc_ref[...] += _dot(act.astype(BF16), w2_ref[0])

    @pl.when(j == nf - 1)
    def _():
        _store_token_major(o_ref, acc_ref[...], tm)


def _experts(block_e, n_used, xs, wgt, wut, bg, bu, w2, b2, tm, tf):
    ne, f, d = wgt.shape
    nch = d // LANES
    p = xs.shape[0] // nch
    nblk = p // tm
    nf = f // tf

    def wtile(i, j, be, nu):
        return (be[i], jnp.where(i < nu[0], j, nf - 1), 0)

    def btile(i, j, be, nu):
        return (be[i], 0, jnp.where(i < nu[0], j, nf - 1))

    grid_spec = pltpu.PrefetchScalarGridSpec(
        num_scalar_prefetch=2,
        grid=(nblk, nf),
        in_specs=[
            pl.BlockSpec((tm * nch, LANES), lambda i, j, be, nu: (i, 0)),
            pl.BlockSpec((1, tf, d), wtile),
            pl.BlockSpec((1, tf, d), wtile),
            pl.BlockSpec((1, 1, tf), btile),
            pl.BlockSpec((1, 1, tf), btile),
            pl.BlockSpec((1, tf, d), wtile),
            pl.BlockSpec((1, 1, d), lambda i, j, be, nu: (be[i], 0, 0)),
        ],
        out_specs=pl.BlockSpec((tm * nch, LANES), lambda i, j, be, nu: (i, 0)),
        scratch_shapes=[pltpu.VMEM((tm, d), BF16), pltpu.VMEM((tm, d), F32)],
    )
    return pl.pallas_call(
        functools.partial(_expert_kernel, tm, nch, nf),
        grid_spec=grid_spec,
        out_shape=jax.ShapeDtypeStruct((p * nch, LANES), F32),
        compiler_params=_cparams(("arbitrary", "arbitrary")),
        name="experts",
    )(block_e, n_used, xs, wgt, wut, bg, bu, w2, b2)


def _split_w1_kernel(f, x_ref, g_ref, u_ref, t_ref):
    xt = x_ref[0].T
    for c in range(t_ref.shape[0]):
        sl = slice(c * LANES, (c + 1) * LANES)
        t_ref[c] = xt[:, sl]
        g_ref[0, :, sl] = t_ref[c, pl.ds(0, f, stride=2), :].astype(BF16)
        u_ref[0, :, sl] = t_ref[c, pl.ds(1, f, stride=2), :].astype(BF16)


def _split_w1(w1, td):
    ne, d, f2 = w1.shape
    f = f2 // 2
    td = min(td, d)
    out = jax.ShapeDtypeStruct((ne, f, d), BF16)
    return pl.pallas_call(
        functools.partial(_split_w1_kernel, f),
        grid=(ne, d // td),
        in_specs=[pl.BlockSpec((1, td, f2), lambda e, i: (e, i, 0))],
        out_specs=[pl.BlockSpec((1, f, td), lambda e, i: (e, 0, i)),
                   pl.BlockSpec((1, f, td), lambda e, i: (e, 0, i))],
        out_shape=[out, out],
        scratch_shapes=[pltpu.VMEM((td // LANES, f2, LANES), F32)],
        compiler_params=_cparams(("arbitrary", "arbitrary")),
        name="split_w1",
    )(w1)


def _final_kernel(d, tm, alpha,
                  dest_ref, x1_ref, mod_ref, gate_ref, src_ref, g2_ref, b2_ref, o_ref,
                  rows_ref, z_ref, sem):
    nch = d // LANES

    def issue(r, carry):
        dst = pl.multiple_of(r * nch, nch)
        for k in range(TOP_K):
            src = pl.multiple_of(dest_ref[r * TOP_K + k] * nch, nch)
            pltpu.make_async_copy(src_ref.at[pl.ds(src, nch)],
                                  rows_ref.at[k, pl.ds(dst, nch)], sem).start()
        return carry

    lax.fori_loop(0, tm, issue, 0, unroll=4)
    for k in range(TOP_K):
        pltpu.make_async_copy(src_ref.at[pl.ds(0, tm * nch)], rows_ref.at[k], sem).wait()
    gate = gate_ref[...]
    mod = mod_ref[0]
    for c in range(nch):
        sl = slice(c * LANES, (c + 1) * LANES)
        y = gate[:, 0:1] * _load_token_major(rows_ref, c, tm, nch, at=(0,))
        for k in range(1, TOP_K):
            y = y + gate[:, k:k + 1] * _load_token_major(rows_ref, c, tm, nch, at=(k,))
        z_ref[:, sl] = alpha * x1_ref[:, sl] + mod[:, 5 * d + c * LANES:5 * d + (c + 1) * LANES] * y
    o_ref[0] = _ln(z_ref[...]) * g2_ref[...] + b2_ref[...]


def _final(dest_flat, x1, mod3, gate, eo, g2, b2, alpha, b0, nbg, s, tm):
    d = x1.shape[1]
    nst = s // tm
    off = b0 * nst
    kern = functools.partial(_final_kernel, d, tm, alpha)
    return pl.pallas_call(
        kern,
        grid=(nbg * nst,),
        in_specs=[
            pl.BlockSpec((tm * TOP_K,), lambda i: (i + off,), memory_space=pltpu.SMEM),
            pl.BlockSpec((tm, d), lambda i: (i + off, 0)),
            pl.BlockSpec((1, 1, 6 * d), lambda i: (b0 + i // nst, 0, 0)),
            pl.BlockSpec((tm, LANES), lambda i: (i + off, 0)),
            pl.BlockSpec(memory_space=pl.ANY),
            pl.BlockSpec((1, d), lambda i: (0, 0)),
            pl.BlockSpec((1, d), lambda i: (0, 0)),
        ],
        out_specs=pl.BlockSpec((1, tm, d), lambda i: (i // nst, i % nst, 0)),
        out_shape=jax.ShapeDtypeStruct((nbg, s, d), F32),
        scratch_shapes=[pltpu.VMEM((TOP_K, tm * (d // LANES), LANES), F32), pltpu.VMEM((tm, d), F32),
                        pltpu.SemaphoreType.DMA(())],
        compiler_params=_cparams(("arbitrary",)),
        name="final_combine",
    )(dest_flat, x1, mod3, gate, eo, g2, b2)


def _rope_tables(s):
    inv = 1.0 / (ROPE_THETA ** (jnp.arange(0, MLA_ROPE, 2, dtype=F32) / MLA_ROPE))
    ang = jnp.arange(s, dtype=F32)[:, None] * inv[None, :]
    cos, sin = jnp.cos(ang), jnp.sin(ang)
    one = jnp.ones((s, MLA_NOPE), F32)
    z32 = jnp.zeros((s, ROPE_HALF), F32)
    ztail = jnp.zeros((s, MLA_QK_PAD - MLA_NOPE - MLA_ROPE), F32)
    znope = jnp.zeros((s, MLA_NOPE), F32)
    c = jnp.concatenate([one, cos, cos, ztail], axis=1)
    sa = jnp.concatenate([znope, z32, sin, ztail], axis=1)
    sb = jnp.concatenate([znope, -sin, z32, ztail], axis=1)
    return c, sa, sb


def _pad_cols(w, width):
    return jnp.pad(w, ((0, 0), (0, width - w.shape[1])))


def _layer(xp, xs, mod3, lp, layer_idx, alpha):
    nbp, s, d = xp.shape
    nb = nbp + xs.shape[0]
    dw = d // 2
    hm = (d - dw) // MLA_V
    ne = lp["w_router"].shape[1]
    f = lp["w_mlp2"].shape[1]
    t = nb * s
    tm = min(256, s)
    tq = min(256, s)

    w_in = lp["w_in"]
    q_lora = lp["mla_q_norm_g"].shape[0]
    kv_lora = lp["mla_kv_norm_g"].shape[0]
    o3, o4, o5 = 3 * dw, 3 * dw + q_lora, 3 * dw + q_lora + kv_lora
    kpe_w = jnp.concatenate([jnp.zeros((d, MLA_NOPE), F32), w_in[:, o5:],
                             jnp.zeros((d, MLA_QK_PAD - MLA_NOPE - MLA_ROPE), F32)], axis=1)
    win = jnp.concatenate([w_in[:, :o5], kpe_w], axis=1).astype(BF16)
    wq = lp["mla_w_q_up"].reshape(q_lora, hm, MLA_NOPE + MLA_ROPE)
    wq = jnp.pad(wq, ((0, 0), (0, 0), (0, MLA_QK_PAD - MLA_NOPE - MLA_ROPE)))
    wq = wq.reshape(q_lora, hm * MLA_QK_PAD).astype(BF16)
    wkv = lp["mla_w_kv_up"].reshape(kv_lora, hm, MLA_NOPE + MLA_V)
    wk = jnp.pad(wkv[:, :, :MLA_NOPE], ((0, 0), (0, 0), (0, MLA_QK_PAD - MLA_NOPE)))
    wk = wk.reshape(kv_lora, hm * MLA_QK_PAD).astype(BF16)
    wv = wkv[:, :, MLA_NOPE:].reshape(kv_lora, hm * MLA_V).astype(BF16)
    rc, rsa, rsb = _rope_tables(s)

    qkv, qm, km, vm = _ln_proj(xp, xs, mod3, win, lp["mla_q_norm_g"].reshape(1, -1), wq,
                               lp["mla_kv_norm_g"].reshape(1, -1), wk, wv, rc, rsa, rsb, tm)

    lam_init = 0.8 - 0.6 * math.exp(-0.3 * layer_idx)
    od = _diff_attn(qkv, lp["diff_lambda_q1"].reshape(1, -1), lp["diff_lambda_k1"].reshape(1, -1),
                    lp["diff_lambda_q2"].reshape(1, -1), lp["diff_lambda_k2"].reshape(1, -1),
                    lp["diff_subln_g"].reshape(1, -1), nb, s, dw, lam_init, tq)
    om = _mla_attn(qm, km, vm, nb, s, tq)

    x1, h2, idx, gate, rank, cnt = _post(
        od, om, xp, xs, mod3, lp["w_out"].astype(BF16), lp["ln1_g"].reshape(1, -1),
        lp["ln1_b"].reshape(1, -1), lp["w_router"].astype(BF16), lp["b_router"].reshape(1, -1),
        alpha, tm)

    te = min(512, t * TOP_K // ne)
    nblk = (t * TOP_K) // te + ne
    p = nblk * te
    counts = cnt[0].astype(I32)
    padded = (counts + te - 1) // te * te
    pends = jnp.cumsum(padded)
    pstarts = pends - padded
    idx4 = idx[:, :TOP_K]
    onehot = idx4[:, :, None] == jnp.arange(ne, dtype=I32)[None, None, :]
    dest = jnp.sum(jnp.where(onehot, pstarts[None, None, :], 0), axis=-1) + rank[:, :TOP_K]
    dest_flat = dest.reshape(-1).astype(I32)
    buf_tok = jnp.zeros((p,), I32).at[dest_flat].set(jnp.arange(t * TOP_K, dtype=I32) // TOP_K)
    n_used = (pends[-1] // te).astype(I32)
    blk = jnp.minimum(jnp.arange(nblk, dtype=I32), n_used - 1)
    block_e = jnp.minimum(jnp.searchsorted(pends, blk * te, side="right"), ne - 1).astype(I32)

    xsorted = _gather_rows(buf_tok, h2, d // LANES, min(256, te))

    b1 = lp["b_mlp1"]
    wgt, wut = _split_w1(lp["w_mlp1"], 256)
    bg = b1[:, 0::2].reshape(ne, 1, f)
    bu = b1[:, 1::2].reshape(ne, 1, f)
    eo = _experts(block_e, n_used.reshape(1), xsorted, wgt, wut, bg, bu, lp["w_mlp2"].astype(BF16),
                  lp["b_mlp2"].reshape(ne, 1, d), te, min(512, f))

    g2 = lp["ln2_g"].reshape(1, -1)
    b2 = lp["ln2_b"].reshape(1, -1)
    yp = _final(dest_flat, x1, mod3, gate, eo, g2, b2, alpha, 0, nbp, s, tm)
    ys = _final(dest_flat, x1, mod3, gate, eo, g2, b2, alpha, nbp, nb - nbp, s, tm)
    return yp, ys


def kernel(x_prompt, x_sample, c_prompt, c_sample, w_ada, b_ada, w_in, diff_lambda_q1, diff_lambda_k1, diff_lambda_q2, diff_lambda_k2, diff_subln_g, mla_q_norm_g, mla_w_q_up, mla_kv_norm_g, mla_w_kv_up, w_out, ln1_g, ln1_b, w_router, b_router, w_mlp1, b_mlp1, w_mlp2, b_mlp2, ln2_g, ln2_b):
    weights = dict(
        w_ada=w_ada, b_ada=b_ada, w_in=w_in, diff_lambda_q1=diff_lambda_q1,
        diff_lambda_k1=diff_lambda_k1, diff_lambda_q2=diff_lambda_q2, diff_lambda_k2=diff_lambda_k2,
        diff_subln_g=diff_subln_g, mla_q_norm_g=mla_q_norm_g, mla_w_q_up=mla_w_q_up,
        mla_kv_norm_g=mla_kv_norm_g, mla_w_kv_up=mla_w_kv_up, w_out=w_out, ln1_g=ln1_g, ln1_b=ln1_b,
        w_router=w_router, b_router=b_router, w_mlp1=w_mlp1, b_mlp1=b_mlp1, w_mlp2=w_mlp2,
        b_mlp2=b_mlp2, ln2_g=ln2_g, ln2_b=ln2_b)
    depth = w_ada.shape[0]
    assert x_prompt.shape[1:] == x_sample.shape[1:], "the two request groups are batched together"
    nbp, nbs = x_prompt.shape[0], x_sample.shape[0]
    nb = nbp + nbs
    d = x_prompt.shape[2]
    alpha = (2.0 * depth) ** 0.25
    bpad = -(-nb // 8) * 8
    c_pad = jnp.pad(jnp.concatenate([c_prompt, c_sample], axis=0), ((0, bpad - nb), (0, 0)))
    xp, xs = x_prompt, x_sample
    for i in range(depth):
        lp = {name: arr[i] for name, arr in weights.items()}
        mod3 = _ada(c_pad, lp["w_ada"], lp["b_ada"]).reshape(bpad, 1, 6 * d)
        xp, xs = _layer(xp, xs, mod3, lp, i, alpha)
    return (xp, xs)
```

```python
import functools
import math

import jax
import jax.numpy as jnp
from jax import lax
from jax.experimental import pallas as pl
from jax.experimental.pallas import tpu as pltpu

F32 = jnp.float32
BF16 = jnp.bfloat16
I32 = jnp.int32

DIFF_HEAD_DIM = 128
MLA_NOPE = 128
MLA_ROPE = 64
MLA_V = 128
ROPE_THETA = 10000.0
TOP_K = 4
SWIGLU_LIMIT = 7.0
SWIGLU_ALPHA = 1.702
LN_EPS = 1e-5
RMS_EPS = 1e-6
LOG2E = math.log2(math.e)

LANES = 128
MLA_QK_PAD = 2 * LANES
ROPE_HALF = MLA_ROPE // 2
VMEM_LIMIT = 56 * 1024 * 1024


def _cparams(sem, vmem=VMEM_LIMIT):
    return pltpu.CompilerParams(dimension_semantics=sem, vmem_limit_bytes=vmem)


def _ln(x):
    mu = jnp.mean(x, axis=-1, keepdims=True)
    xc = x - mu
    var = jnp.mean(xc * xc, axis=-1, keepdims=True)
    return xc * lax.rsqrt(var + LN_EPS)


def _rms(x, g):
    return x * lax.rsqrt(jnp.mean(x * x, axis=-1, keepdims=True) + RMS_EPS) * g


def _dot(a, b):
    return jnp.dot(a, b, preferred_element_type=F32)


def _dot_nt(a, b):
    return lax.dot_general(a, b, (((1,), (1,)), ((), ())), preferred_element_type=F32)


def _softmax_pv(s2, v):
    m = jnp.max(s2, axis=-1, keepdims=True)
    e = jnp.exp2(s2 - m)
    inv = 1.0 / jnp.sum(e, axis=-1, keepdims=True)
    return _dot(e.astype(BF16), v) * inv


def _ada_kernel(c_ref, w_ref, b_ref, o_ref):
    c = c_ref[...]
    s = c * jax.nn.sigmoid(c)
    o_ref[...] = _dot(s.astype(BF16), w_ref[...].astype(BF16)) + b_ref[...]


def _ada(c_pad, w_ada, b_ada, tn=1024):
    bp, d = c_pad.shape
    n = w_ada.shape[1]
    tn = min(tn, n)
    return pl.pallas_call(
        _ada_kernel,
        grid=(n // tn,),
        in_specs=[pl.BlockSpec((bp, d), lambda j: (0, 0)),
                  pl.BlockSpec((d, tn), lambda j: (0, j)),
                  pl.BlockSpec((1, tn), lambda j: (0, j))],
        out_specs=pl.BlockSpec((bp, tn), lambda j: (0, j)),
        out_shape=jax.ShapeDtypeStruct((bp, n), F32),
        compiler_params=_cparams(("arbitrary",)),
        name="ada",
    )(c_pad, w_ada, b_ada.reshape(1, n))


def _rope_pad(x, c, sa, sb):
    return (x * c + pltpu.roll(x, ROPE_HALF, 1) * sa
            + pltpu.roll(x, MLA_QK_PAD - ROPE_HALF, 1) * sb)


def _ln_proj_kernel(nbp, d, dw, q_lora, kv_lora, hm,
                    xp_ref, xs_ref, mod_ref, win_ref, gq_ref, wq_ref, gkv_ref, wk_ref, wv_ref,
                    c_ref, sa_ref, sb_ref,
                    qkv_ref, qm_ref, km_ref, vm_ref):
    b = pl.program_id(0)
    x = jnp.where(b < nbp, xp_ref[0], xs_ref[0])
    mod = mod_ref[0]
    h = _ln(x) * (1.0 + mod[:, d:2 * d]) + mod[:, 0:d]
    hb = h.astype(BF16)
    o3 = 3 * dw
    qkv_ref[...] = _dot(hb, win_ref[:, 0:o3]).astype(BF16)
    cq = _dot(hb, win_ref[:, o3:o3 + q_lora])
    ckv = _dot(hb, win_ref[:, o3 + q_lora:o3 + q_lora + kv_lora])
    kpe = _dot(hb, win_ref[:, o3 + q_lora + kv_lora:o3 + q_lora + kv_lora + MLA_QK_PAD])
    c, sa, sb = c_ref[...], sa_ref[...], sb_ref[...]
    q = _dot(_rms(cq, gq_ref[...]).astype(BF16), wq_ref[...])
    ckvn = _rms(ckv, gkv_ref[...]).astype(BF16)
    kk = _dot(ckvn, wk_ref[...])
    vm_ref[...] = _dot(ckvn, wv_ref[...]).astype(BF16)
    kp = _rope_pad(kpe, c, sa, sb)
    for hh in range(hm):
        sl = slice(hh * MLA_QK_PAD, (hh + 1) * MLA_QK_PAD)
        qm_ref[:, sl] = _rope_pad(q[:, sl], c, sa, sb).astype(BF16)
        km_ref[:, sl] = (kk[:, sl] + kp).astype(BF16)


def _ln_proj(xp, xs, mod3, win, gq, wq, gkv, wk, wv, rc, rsa, rsb, tm):
    nbp, s, d = xp.shape
    nbs = xs.shape[0]
    nb = nbp + nbs
    dw = d // 2
    q_lora, kv_lora = gq.shape[1], gkv.shape[1]
    hm = wv.shape[1] // MLA_V
    nst = s // tm
    t = nb * s
    const = lambda bb, ss: (0, 0)
    row = lambda bb, ss: (bb * nst + ss, 0)
    once = dict(pipeline_mode=pl.Buffered(1))
    kern = functools.partial(_ln_proj_kernel, nbp, d, dw, q_lora, kv_lora, hm)
    return pl.pallas_call(
        kern,
        grid=(nb, nst),
        in_specs=[
            pl.BlockSpec((1, tm, d), lambda bb, ss: (jnp.minimum(bb, nbp - 1), ss, 0)),
            pl.BlockSpec((1, tm, d), lambda bb, ss: (jnp.maximum(bb - nbp, 0), ss, 0)),
            pl.BlockSpec((1, 1, 6 * d), lambda bb, ss: (bb, 0, 0)),
            pl.BlockSpec(win.shape, const, **once),
            pl.BlockSpec(gq.shape, const),
            pl.BlockSpec(wq.shape, const, **once),
            pl.BlockSpec(gkv.shape, const),
            pl.BlockSpec(wk.shape, const, **once),
            pl.BlockSpec(wv.shape, const, **once),
            pl.BlockSpec((tm, MLA_QK_PAD), lambda bb, ss: (ss, 0)),
            pl.BlockSpec((tm, MLA_QK_PAD), lambda bb, ss: (ss, 0)),
            pl.BlockSpec((tm, MLA_QK_PAD), lambda bb, ss: (ss, 0)),
        ],
        out_specs=[
            pl.BlockSpec((tm, 3 * dw), row),
            pl.BlockSpec((tm, hm * MLA_QK_PAD), row),
            pl.BlockSpec((tm, hm * MLA_QK_PAD), row),
            pl.BlockSpec((tm, hm * MLA_V), row),
        ],
        out_shape=[
            jax.ShapeDtypeStruct((t, 3 * dw), BF16),
            jax.ShapeDtypeStruct((t, hm * MLA_QK_PAD), BF16),
            jax.ShapeDtypeStruct((t, hm * MLA_QK_PAD), BF16),
            jax.ShapeDtypeStruct((t, hm * MLA_V), BF16),
        ],
        compiler_params=_cparams(("arbitrary", "arbitrary")),
        name="ln_proj",
    )(xp, xs, mod3, win, gq, wq, gkv, wk, wv, rc, rsa, rsb)


def _diff_kernel(hd, tq, s, lam_init,
                 q_ref, k_ref, v_ref, lq1_ref, lk1_ref, lq2_ref, lk2_ref, g_ref, o_ref):
    lam = (jnp.exp(jnp.sum(lq1_ref[...] * lk1_ref[...], axis=-1, keepdims=True))
           - jnp.exp(jnp.sum(lq2_ref[...] * lk2_ref[...], axis=-1, keepdims=True)) + lam_init)
    q0 = pl.program_id(1) * tq
    rowp = lax.broadcasted_iota(I32, (tq, s), 0) + q0
    colp = lax.broadcasted_iota(I32, (tq, s), 1)
    dist = jnp.abs(rowp - colp).astype(F32)
    dv = 2 * DIFF_HEAD_DIM
    g = g_ref[...]
    for h in range(hd):
        bias = dist * (LOG2E * 2.0 ** (-8.0 * (h + 1) / hd))
        v = v_ref[:, h * dv:(h + 1) * dv]
        om = []
        for m in range(2):
            sl = slice(h * dv + m * DIFF_HEAD_DIM, h * dv + (m + 1) * DIFF_HEAD_DIM)
            om.append(_softmax_pv(_dot_nt(q_ref[:, sl], k_ref[:, sl]) - bias, v))
        o = om[0] - lam * om[1]
        o_ref[:, h * dv:(h + 1) * dv] = (_rms(o, g) * (1.0 - lam_init)).astype(BF16)


def _diff_attn(qkv, lq1, lk1, lq2, lk2, g, nb, s, dw, lam_init, tq):
    hd = dw // (2 * DIFF_HEAD_DIM)
    nq = s // tq
    vec = pl.BlockSpec((1, DIFF_HEAD_DIM), lambda b, i: (0, 0))
    kern = functools.partial(_diff_kernel, hd, tq, s, lam_init)
    return pl.pallas_call(
        kern,
        grid=(nb, nq),
        in_specs=[pl.BlockSpec((tq, dw), lambda b, i: (b * nq + i, 0)),
                  pl.BlockSpec((s, dw), lambda b, i: (b, 1)),
                  pl.BlockSpec((s, dw), lambda b, i: (b, 2)),
                  vec, vec, vec, vec,
                  pl.BlockSpec((1, 2 * DIFF_HEAD_DIM), lambda b, i: (0, 0))],
        out_specs=pl.BlockSpec((tq, dw), lambda b, i: (b * nq + i, 0)),
        out_shape=jax.ShapeDtypeStruct((nb * s, dw), BF16),
        compiler_params=_cparams(("arbitrary", "arbitrary")),
        name="diff_attn",
    )(qkv, qkv, qkv, lq1, lk1, lq2, lk2, g)


def _mla_kernel(hm, q_ref, k_ref, v_ref, o_ref):
    for h in range(hm):
        sl = slice(h * MLA_QK_PAD, (h + 1) * MLA_QK_PAD)
        o_ref[:, h * MLA_V:(h + 1) * MLA_V] = _softmax_pv(
            _dot_nt(q_ref[:, sl], k_ref[:, sl]), v_ref[:, h * MLA_V:(h + 1) * MLA_V]).astype(BF16)


def _mla_attn(qm, km, vm, nb, s, tq):
    hm = vm.shape[1] // MLA_V
    nq = s // tq
    return pl.pallas_call(
        functools.partial(_mla_kernel, hm),
        grid=(nb, nq),
        in_specs=[pl.BlockSpec((tq, hm * MLA_QK_PAD), lambda b, i: (b * nq + i, 0)),
                  pl.BlockSpec((s, hm * MLA_QK_PAD), lambda b, i: (b, 0)),
                  pl.BlockSpec((s, hm * MLA_V), lambda b, i: (b, 0))],
        out_specs=pl.BlockSpec((tq, hm * MLA_V), lambda b, i: (b * nq + i, 0)),
        out_shape=jax.ShapeDtypeStruct((nb * s, hm * MLA_V), BF16),
        compiler_params=_cparams(("arbitrary", "arbitrary")),
        name="mla_attn",
    )(qm, km, vm)


def _post_kernel(nbp, d, dw, ne, tm, alpha,
                 od_ref, om_ref, xp_ref, xs_ref, mod_ref, wo_ref, g1_ref, b1_ref, wr_ref, br_ref,
                 x1_ref, h2_ref, idx_ref, gate_ref, rank_ref, cnt_ref, run_ref):
    b = pl.program_id(0)
    first = jnp.logical_and(b == 0, pl.program_id(1) == 0)

    @pl.when(first)
    def _():
        run_ref[...] = jnp.zeros_like(run_ref)

    mix = _dot(od_ref[...], wo_ref[0:dw, :]) + _dot(om_ref[...], wo_ref[dw:2 * dw, :])
    x = jnp.where(b < nbp, xp_ref[0], xs_ref[0])
    mod = mod_ref[0]
    x1 = _ln(alpha * x + mod[:, 2 * d:3 * d] * mix) * g1_ref[...] + b1_ref[...]
    x1_ref[...] = x1
    h2 = _ln(x1) * (1.0 + mod[:, 4 * d:5 * d]) + mod[:, 3 * d:4 * d]
    h2_ref[...] = h2
    logits = _dot(h2.astype(BF16), wr_ref[...]) + br_ref[...]

    lane = lax.broadcasted_iota(I32, (tm, ne), 1).astype(F32)
    work = logits
    vals, idxs = [], []
    for _ in range(TOP_K):
        m = jnp.max(work, axis=-1, keepdims=True)
        i = jnp.min(jnp.where(work == m, lane, float(ne)), axis=-1, keepdims=True)
        vals.append(m)
        idxs.append(i)
        work = jnp.where(lane == i, -jnp.inf, work)
    es = [jnp.exp(v - vals[0]) for v in vals]
    inv = 1.0 / (es[0] + es[1] + es[2] + es[3])

    ohs = [jnp.where(lane == i, 1.0, 0.0) for i in idxs]
    oh = ohs[0] + ohs[1] + ohs[2] + ohs[3]
    tri = jnp.where(lax.broadcasted_iota(I32, (tm, tm), 0) > lax.broadcasted_iota(I32, (tm, tm), 1),
                    1.0, 0.0)
    before = _dot(tri.astype(BF16), oh.astype(BF16)) + run_ref[...]
    run_ref[...] = run_ref[...] + jnp.sum(oh, axis=0, keepdims=True)
    cnt_ref[...] = run_ref[...]

    lane_o = lax.broadcasted_iota(I32, (tm, LANES), 1)
    idx_o = jnp.zeros((tm, LANES), I32)
    gate_o = jnp.zeros((tm, LANES), F32)
    rank_o = jnp.zeros((tm, LANES), I32)
    for k in range(TOP_K):
        rk = jnp.sum(ohs[k] * before, axis=-1, keepdims=True).astype(I32)
        idx_o = jnp.where(lane_o == k, idxs[k].astype(I32), idx_o)
        gate_o = jnp.where(lane_o == k, es[k] * inv, gate_o)
        rank_o = jnp.where(lane_o == k, rk, rank_o)
    idx_ref[...] = idx_o
    gate_ref[...] = gate_o
    rank_ref[...] = rank_o


def _post(od, om, xp, xs, mod3, wo, g1, b1, wr, br, alpha, tm):
    nbp, s, d = xp.shape
    nb = nbp + xs.shape[0]
    dw = d // 2
    ne = wr.shape[1]
    nst = s // tm
    t = nb * s
    const = lambda bb, ss: (0, 0)
    row = lambda bb, ss: (bb * nst + ss, 0)
    kern = functools.partial(_post_kernel, nbp, d, dw, ne, tm, alpha)
    return pl.pallas_call(
        kern,
        grid=(nb, nst),
        in_specs=[
            pl.BlockSpec((tm, dw), row),
            pl.BlockSpec((tm, dw), row),
            pl.BlockSpec((1, tm, d), lambda bb, ss: (jnp.minimum(bb, nbp - 1), ss, 0)),
            pl.BlockSpec((1, tm, d), lambda bb, ss: (jnp.maximum(bb - nbp, 0), ss, 0)),
            pl.BlockSpec((1, 1, 6 * d), lambda bb, ss: (bb, 0, 0)),
            pl.BlockSpec(wo.shape, const, pipeline_mode=pl.Buffered(1)),
            pl.BlockSpec((1, d), const),
            pl.BlockSpec((1, d), const),
            pl.BlockSpec(wr.shape, const),
            pl.BlockSpec((1, ne), const),
        ],
        out_specs=[
            pl.BlockSpec((tm, d), row),
            pl.BlockSpec((tm, d), row),
            pl.BlockSpec((tm, LANES), row),
            pl.BlockSpec((tm, LANES), row),
            pl.BlockSpec((tm, LANES), row),
            pl.BlockSpec((1, ne), const),
        ],
        out_shape=[
            jax.ShapeDtypeStruct((t, d), F32),
            jax.ShapeDtypeStruct((t, d), F32),
            jax.ShapeDtypeStruct((t, LANES), I32),
            jax.ShapeDtypeStruct((t, LANES), F32),
            jax.ShapeDtypeStruct((t, LANES), I32),
            jax.ShapeDtypeStruct((1, ne), F32),
        ],
        scratch_shapes=[pltpu.VMEM((1, ne), F32)],
        compiler_params=_cparams(("arbitrary", "arbitrary")),
        name="post_mixer",
    )(od, om, xp, xs, mod3, wo, g1, b1, wr, br)


def _gather_kernel(tg, tok_ref, src_ref, out_ref, sem):
    def issue(r, carry):
        pltpu.make_async_copy(src_ref.at[pl.ds(tok_ref[r], 1)], out_ref.at[pl.ds(r, 1)], sem).start()
        return carry

    lax.fori_loop(0, tg, issue, 0, unroll=8)
    pltpu.make_async_copy(src_ref.at[pl.ds(0, tg)], out_ref, sem).wait()


def _gather_rows(tok, src, tg):
    p = tok.shape[0]
    d = src.shape[1]
    return pl.pallas_call(
        functools.partial(_gather_kernel, tg),
        grid=(p // tg,),
        in_specs=[pl.BlockSpec((tg,), lambda i: (i,), memory_space=pltpu.SMEM),
                  pl.BlockSpec(memory_space=pl.ANY)],
        out_specs=pl.BlockSpec((tg, d), lambda i: (i, 0)),
        out_shape=jax.ShapeDtypeStruct((p, d), src.dtype),
        scratch_shapes=[pltpu.SemaphoreType.DMA(())],
        compiler_params=_cparams(("arbitrary",)),
        name="gather_rows",
    )(tok, src)


def _expert_kernel(be_ref, nu_ref, x_ref, wg_ref, wu_ref, bg_ref, bu_ref, w2_ref, b2_ref,
                   o_ref, xb_ref):
    i = pl.program_id(0)
    j = pl.program_id(1)
    used = i < nu_ref[0]

    @pl.when(j == 0)
    def _():
        xb_ref[...] = x_ref[...].astype(BF16)
        o_ref[...] = jnp.broadcast_to(b2_ref[0], o_ref.shape)

    @pl.when(used)
    def _():
        xb = xb_ref[...]
        hg = _dot_nt(xb, wg_ref[0]) + bg_ref[0]
        hu = _dot_nt(xb, wu_ref[0]) + bu_ref[0]
        glu = jnp.minimum(hg, SWIGLU_LIMIT)
        lin = jnp.clip(hu, -SWIGLU_LIMIT, SWIGLU_LIMIT)
        act = glu * jax.nn.sigmoid(SWIGLU_ALPHA * glu) * (lin + 1.0)
        o_ref[...] += _dot(act.astype(BF16), w2_ref[0])


def _experts(block_e, n_used, xs, wgt, wut, bg, bu, w2, b2, tm, tf):
    ne, f, d = wgt.shape
    p = xs.shape[0]
    nblk = p // tm
    nf = f // tf

    def wtile(i, j, be, nu):
        return (be[i], jnp.where(i < nu[0], j, nf - 1), 0)

    def btile(i, j, be, nu):
        return (be[i], 0, jnp.where(i < nu[0], j, nf - 1))

    grid_spec = pltpu.PrefetchScalarGridSpec(
        num_scalar_prefetch=2,
        grid=(nblk, nf),
        in_specs=[
            pl.BlockSpec((tm, d), lambda i, j, be, nu: (i, 0)),
            pl.BlockSpec((1, tf, d), wtile),
            pl.BlockSpec((1, tf, d), wtile),
            pl.BlockSpec((1, 1, tf), btile),
            pl.BlockSpec((1, 1, tf), btile),
            pl.BlockSpec((1, tf, d), wtile),
            pl.BlockSpec((1, 1, d), lambda i, j, be, nu: (be[i], 0, 0)),
        ],
        out_specs=pl.BlockSpec((tm, d), lambda i, j, be, nu: (i, 0)),
        scratch_shapes=[pltpu.VMEM((tm, d), BF16)],
    )
    return pl.pallas_call(
        _expert_kernel,
        grid_spec=grid_spec,
        out_shape=jax.ShapeDtypeStruct((p, d), F32),
        compiler_params=_cparams(("arbitrary", "arbitrary")),
        name="experts",
    )(block_e, n_used, xs, wgt, wut, bg, bu, w2, b2)


def _split_w1_kernel(f, x_ref, g_ref, u_ref, t_ref):
    xt = x_ref[0].T
    for c in range(t_ref.shape[0]):
        sl = slice(c * LANES, (c + 1) * LANES)
        t_ref[c] = xt[:, sl]
        g_ref[0, :, sl] = t_ref[c, pl.ds(0, f, stride=2), :].astype(BF16)
        u_ref[0, :, sl] = t_ref[c, pl.ds(1, f, stride=2), :].astype(BF16)


def _split_w1(w1, td):
    ne, d, f2 = w1.shape
    f = f2 // 2
    td = min(td, d)
    out = jax.ShapeDtypeStruct((ne, f, d), BF16)
    return pl.pallas_call(
        functools.partial(_split_w1_kernel, f),
        grid=(ne, d // td),
        in_specs=[pl.BlockSpec((1, td, f2), lambda e, i: (e, i, 0))],
        out_specs=[pl.BlockSpec((1, f, td), lambda e, i: (e, 0, i)),
                   pl.BlockSpec((1, f, td), lambda e, i: (e, 0, i))],
        out_shape=[out, out],
        scratch_shapes=[pltpu.VMEM((td // LANES, f2, LANES), F32)],
        compiler_params=_cparams(("arbitrary", "arbitrary")),
        name="split_w1",
    )(w1)


def _final_kernel(d, tm, alpha,
                  dest_ref, x1_ref, mod_ref, gate_ref, src_ref, g2_ref, b2_ref, o_ref,
                  rows_ref, sem):
    def issue(r, carry):
        for k in range(TOP_K):
            pltpu.make_async_copy(src_ref.at[pl.ds(dest_ref[r * TOP_K + k], 1)],
                                  rows_ref.at[k, pl.ds(r, 1)], sem).start()
        return carry

    lax.fori_loop(0, tm, issue, 0, unroll=4)
    for k in range(TOP_K):
        pltpu.make_async_copy(src_ref.at[pl.ds(0, tm)], rows_ref.at[k], sem).wait()
    gate = gate_ref[...]
    y = gate[:, 0:1] * rows_ref[0]
    for k in range(1, TOP_K):
        y = y + gate[:, k:k + 1] * rows_ref[k]
    mod = mod_ref[0]
    z = alpha * x1_ref[...] + mod[:, 5 * d:6 * d] * y
    o_ref[0] = _ln(z) * g2_ref[...] + b2_ref[...]


def _final(dest_flat, x1, mod3, gate, eo, g2, b2, alpha, b0, nbg, s, tm):
    d = x1.shape[1]
    nst = s // tm
    off = b0 * nst
    kern = functools.partial(_final_kernel, d, tm, alpha)
    return pl.pallas_call(
        kern,
        grid=(nbg * nst,),
        in_specs=[
            pl.BlockSpec((tm * TOP_K,), lambda i: (i + off,), memory_space=pltpu.SMEM),
            pl.BlockSpec((tm, d), lambda i: (i + off, 0)),
            pl.BlockSpec((1, 1, 6 * d), lambda i: (b0 + i // nst, 0, 0)),
            pl.BlockSpec((tm, LANES), lambda i: (i + off, 0)),
            pl.BlockSpec(memory_space=pl.ANY),
            pl.BlockSpec((1, d), lambda i: (0, 0)),
            pl.BlockSpec((1, d), lambda i: (0, 0)),
        ],
        out_specs=pl.BlockSpec((1, tm, d), lambda i: (i // nst, i % nst, 0)),
        out_shape=jax.ShapeDtypeStruct((nbg, s, d), F32),
        scratch_shapes=[pltpu.VMEM((TOP_K, tm, d), F32), pltpu.SemaphoreType.DMA(())],
        compiler_params=_cparams(("arbitrary",)),
        name="final_combine",
    )(dest_flat, x1, mod3, gate, eo, g2, b2)


def _rope_tables(s):
    inv = 1.0 / (ROPE_THETA ** (jnp.arange(0, MLA_ROPE, 2, dtype=F32) / MLA_ROPE))
    ang = jnp.arange(s, dtype=F32)[:, None] * inv[None, :]
    cos, sin = jnp.cos(ang), jnp.sin(ang)
    one = jnp.ones((s, MLA_NOPE), F32)
    z32 = jnp.zeros((s, ROPE_HALF), F32)
    ztail = jnp.zeros((s, MLA_QK_PAD - MLA_NOPE - MLA_ROPE), F32)
    znope = jnp.zeros((s, MLA_NOPE), F32)
    c = jnp.concatenate([one, cos, cos, ztail], axis=1)
    sa = jnp.concatenate([znope, z32, sin, ztail], axis=1)
    sb = jnp.concatenate([znope, -sin, z32, ztail], axis=1)
    return c, sa, sb


def _pad_cols(w, width):
    return jnp.pad(w, ((0, 0), (0, width - w.shape[1])))


def _layer(xp, xs, mod3, lp, layer_idx, alpha):
    nbp, s, d = xp.shape
    nb = nbp + xs.shape[0]
    dw = d // 2
    hm = (d - dw) // MLA_V
    ne = lp["w_router"].shape[1]
    f = lp["w_mlp2"].shape[1]
    t = nb * s
    tm = min(256, s)
    tq = min(512, s)

    w_in = lp["w_in"]
    q_lora = lp["mla_q_norm_g"].shape[0]
    kv_lora = lp["mla_kv_norm_g"].shape[0]
    o3, o4, o5 = 3 * dw, 3 * dw + q_lora, 3 * dw + q_lora + kv_lora
    kpe_w = jnp.concatenate([jnp.zeros((d, MLA_NOPE), F32), w_in[:, o5:],
                             jnp.zeros((d, MLA_QK_PAD - MLA_NOPE - MLA_ROPE), F32)], axis=1)
    dq_w = w_in[:, :dw] * (DIFF_HEAD_DIM ** -0.5 * LOG2E)
    win = jnp.concatenate([dq_w, w_in[:, dw:o5], kpe_w], axis=1).astype(BF16)
    wq = lp["mla_w_q_up"] * ((MLA_NOPE + MLA_ROPE) ** -0.5 * LOG2E)
    wq = wq.reshape(q_lora, hm, MLA_NOPE + MLA_ROPE)
    wq = jnp.pad(wq, ((0, 0), (0, 0), (0, MLA_QK_PAD - MLA_NOPE - MLA_ROPE)))
    wq = wq.reshape(q_lora, hm * MLA_QK_PAD).astype(BF16)
    wkv = lp["mla_w_kv_up"].reshape(kv_lora, hm, MLA_NOPE + MLA_V)
    wk = jnp.pad(wkv[:, :, :MLA_NOPE], ((0, 0), (0, 0), (0, MLA_QK_PAD - MLA_NOPE)))
    wk = wk.reshape(kv_lora, hm * MLA_QK_PAD).astype(BF16)
    wv = wkv[:, :, MLA_NOPE:].reshape(kv_lora, hm * MLA_V).astype(BF16)
    rc, rsa, rsb = _rope_tables(s)

    qkv, qm, km, vm = _ln_proj(xp, xs, mod3, win, lp["mla_q_norm_g"].reshape(1, -1), wq,
                               lp["mla_kv_norm_g"].reshape(1, -1), wk, wv, rc, rsa, rsb, tm)

    lam_init = 0.8 - 0.6 * math.exp(-0.3 * layer_idx)
    od = _diff_attn(qkv, lp["diff_lambda_q1"].reshape(1, -1), lp["diff_lambda_k1"].reshape(1, -1),
                    lp["diff_lambda_q2"].reshape(1, -1), lp["diff_lambda_k2"].reshape(1, -1),
                    lp["diff_subln_g"].reshape(1, -1), nb, s, dw, lam_init, tq)
    om = _mla_attn(qm, km, vm, nb, s, tq)

    x1, h2, idx, gate, rank, cnt = _post(
        od, om, xp, xs, mod3, lp["w_out"].astype(BF16), lp["ln1_g"].reshape(1, -1),
        lp["ln1_b"].reshape(1, -1), lp["w_router"].astype(BF16), lp["b_router"].reshape(1, -1),
        alpha, tm)

    te = min(512, t * TOP_K // ne)
    nblk = (t * TOP_K) // te + ne
    p = nblk * te
    counts = cnt[0].astype(I32)
    padded = (counts + te - 1) // te * te
    pends = jnp.cumsum(padded)
    pstarts = pends - padded
    idx4 = idx[:, :TOP_K]
    onehot = idx4[:, :, None] == jnp.arange(ne, dtype=I32)[None, None, :]
    dest = jnp.sum(jnp.where(onehot, pstarts[None, None, :], 0), axis=-1) + rank[:, :TOP_K]
    dest_flat = dest.reshape(-1).astype(I32)
    buf_tok = jnp.zeros((p,), I32).at[dest_flat].set(jnp.arange(t * TOP_K, dtype=I32) // TOP_K)
    n_used = (pends[-1] // te).astype(I32)
    blk = jnp.minimum(jnp.arange(nblk, dtype=I32), n_used - 1)
    block_e = jnp.minimum(jnp.sum((pends[None, :] <= (blk * te)[:, None]).astype(I32), axis=1), ne - 1)

    xsorted = _gather_rows(buf_tok, h2, te)

    b1 = lp["b_mlp1"]
    wgt, wut = _split_w1(lp["w_mlp1"], 256)
    bg = b1[:, 0::2].reshape(ne, 1, f)
    bu = b1[:, 1::2].reshape(ne, 1, f)
    eo = _experts(block_e, n_used.reshape(1), xsorted, wgt, wut, bg, bu, lp["w_mlp2"].astype(BF16),
                  lp["b_mlp2"].reshape(ne, 1, d), te, min(512, f))

    g2 = lp["ln2_g"].reshape(1, -1)
    b2 = lp["ln2_b"].reshape(1, -1)
    yp = _final(dest_flat, x1, mod3, gate, eo, g2, b2, alpha, 0, nbp, s, tm)
    ys = _final(dest_flat, x1, mod3, gate, eo, g2, b2, alpha, nbp, nb - nbp, s, tm)
    return yp, ys


def kernel(x_prompt, x_sample, c_prompt, c_sample, w_ada, b_ada, w_in, diff_lambda_q1, diff_lambda_k1, diff_lambda_q2, diff_lambda_k2, diff_subln_g, mla_q_norm_g, mla_w_q_up, mla_kv_norm_g, mla_w_kv_up, w_out, ln1_g, ln1_b, w_router, b_router, w_mlp1, b_mlp1, w_mlp2, b_mlp2, ln2_g, ln2_b):
    weights = dict(
        w_ada=w_ada, b_ada=b_ada, w_in=w_in, diff_lambda_q1=diff_lambda_q1,
        diff_lambda_k1=diff_lambda_k1, diff_lambda_q2=diff_lambda_q2, diff_lambda_k2=diff_lambda_k2,
        diff_subln_g=diff_subln_g, mla_q_norm_g=mla_q_norm_g, mla_w_q_up=mla_w_q_up,
        mla_kv_norm_g=mla_kv_norm_g, mla_w_kv_up=mla_w_kv_up, w_out=w_out, ln1_g=ln1_g, ln1_b=ln1_b,
        w_router=w_router, b_router=b_router, w_mlp1=w_mlp1, b_mlp1=b_mlp1, w_mlp2=w_mlp2,
        b_mlp2=b_mlp2, ln2_g=ln2_g, ln2_b=ln2_b)
    depth = w_ada.shape[0]
    assert x_prompt.shape[1:] == x_sample.shape[1:], "the two request groups are batched together"
    nbp, nbs = x_prompt.shape[0], x_sample.shape[0]
    nb = nbp + nbs
    d = x_prompt.shape[2]
    alpha = (2.0 * depth) ** 0.25
    bpad = -(-nb // 8) * 8
    c_pad = jnp.pad(jnp.concatenate([c_prompt, c_sample], axis=0), ((0, bpad - nb), (0, 0)))
    xp, xs = x_prompt, x_sample
    for i in range(depth):
        lp = {name: arr[i] for name, arr in weights.items()}
        mod3 = _ada(c_pad, lp["w_ada"], lp["b_ada"]).reshape(bpad, 1, 6 * d)
        xp, xs = _layer(xp, xs, mod3, lp, i, alpha)
    return (xp, xs)
```

```python
import functools
import math

import jax
import jax.numpy as jnp
from jax import lax
from jax.experimental import pallas as pl
from jax.experimental.pallas import tpu as pltpu

F32 = jnp.float32
BF16 = jnp.bfloat16
I32 = jnp.int32

DIFF_HEAD_DIM = 128
MLA_NOPE = 128
MLA_ROPE = 64
MLA_V = 128
ROPE_THETA = 10000.0
TOP_K = 4
SWIGLU_LIMIT = 7.0
SWIGLU_ALPHA = 1.702
LN_EPS = 1e-5
RMS_EPS = 1e-6
LOG2E = math.log2(math.e)

LANES = 128
MLA_QK_PAD = 2 * LANES
ROPE_HALF = MLA_ROPE // 2
VMEM_LIMIT = 56 * 1024 * 1024


def _cparams(sem, vmem=VMEM_LIMIT):
    return pltpu.CompilerParams(dimension_semantics=sem, vmem_limit_bytes=vmem)


def _ln(x):
    mu = jnp.mean(x, axis=-1, keepdims=True)
    xc = x - mu
    var = jnp.mean(xc * xc, axis=-1, keepdims=True)
    return xc * lax.rsqrt(var + LN_EPS)


def _rms(x, g):
    return x * lax.rsqrt(jnp.mean(x * x, axis=-1, keepdims=True) + RMS_EPS) * g


def _dot(a, b):
    return jnp.dot(a, b, preferred_element_type=F32)


def _dot_nt(a, b):
    return lax.dot_general(a, b, (((1,), (1,)), ((), ())), preferred_element_type=F32)


def _softmax_pv(s2, v):
    m = jnp.max(s2, axis=-1, keepdims=True)
    e = jnp.exp2(s2 - m)
    inv = 1.0 / jnp.sum(e, axis=-1, keepdims=True)
    return _dot(e.astype(BF16), v) * inv


def _ada_kernel(c_ref, w_ref, b_ref, o_ref):
    c = c_ref[...]
    s = c * jax.nn.sigmoid(c)
    o_ref[...] = _dot(s.astype(BF16), w_ref[...].astype(BF16)) + b_ref[...]


def _ada(c_pad, w_ada, b_ada, tn=1024):
    bp, d = c_pad.shape
    n = w_ada.shape[1]
    tn = min(tn, n)
    return pl.pallas_call(
        _ada_kernel,
        grid=(n // tn,),
        in_specs=[pl.BlockSpec((bp, d), lambda j: (0, 0)),
                  pl.BlockSpec((d, tn), lambda j: (0, j)),
                  pl.BlockSpec((1, tn), lambda j: (0, j))],
        out_specs=pl.BlockSpec((bp, tn), lambda j: (0, j)),
        out_shape=jax.ShapeDtypeStruct((bp, n), F32),
        compiler_params=_cparams(("arbitrary",)),
        name="ada",
    )(c_pad, w_ada, b_ada.reshape(1, n))


def _rope_pad(x, c, sa, sb):
    return (x * c + pltpu.roll(x, ROPE_HALF, 1) * sa
            + pltpu.roll(x, MLA_QK_PAD - ROPE_HALF, 1) * sb)


def _ln_proj_kernel(nbp, d, dw, q_lora, kv_lora, hm,
                    xp_ref, xs_ref, mod_ref, win_ref, gq_ref, wq_ref, gkv_ref, wk_ref, wv_ref,
                    c_ref, sa_ref, sb_ref,
                    qkv_ref, qm_ref, km_ref, vm_ref):
    b = pl.program_id(0)
    x = jnp.where(b < nbp, xp_ref[0], xs_ref[0])
    mod = mod_ref[0]
    h = _ln(x) * (1.0 + mod[:, d:2 * d]) + mod[:, 0:d]
    hb = h.astype(BF16)
    o3 = 3 * dw
    qkv_ref[...] = _dot(hb, win_ref[:, 0:o3]).astype(BF16)
    cq = _dot(hb, win_ref[:, o3:o3 + q_lora])
    ckv = _dot(hb, win_ref[:, o3 + q_lora:o3 + q_lora + kv_lora])
    kpe = _dot(hb, win_ref[:, o3 + q_lora + kv_lora:o3 + q_lora + kv_lora + MLA_QK_PAD])
    c, sa, sb = c_ref[...], sa_ref[...], sb_ref[...]
    q = _dot(_rms(cq, gq_ref[...]).astype(BF16), wq_ref[...])
    ckvn = _rms(ckv, gkv_ref[...]).astype(BF16)
    kk = _dot(ckvn, wk_ref[...])
    vm_ref[...] = _dot(ckvn, wv_ref[...]).astype(BF16)
    kp = _rope_pad(kpe, c, sa, sb)
    for hh in range(hm):
        sl = slice(hh * MLA_QK_PAD, (hh + 1) * MLA_QK_PAD)
        qm_ref[:, sl] = _rope_pad(q[:, sl], c, sa, sb).astype(BF16)
        km_ref[:, sl] = (kk[:, sl] + kp).astype(BF16)


def _ln_proj(xp, xs, mod3, win, gq, wq, gkv, wk, wv, rc, rsa, rsb, tm):
    nbp, s, d = xp.shape
    nbs = xs.shape[0]
    nb = nbp + nbs
    dw = d // 2
    q_lora, kv_lora = gq.shape[1], gkv.shape[1]
    hm = wv.shape[1] // MLA_V
    nst = s // tm
    t = nb * s
    const = lambda bb, ss: (0, 0)
    row = lambda bb, ss: (bb * nst + ss, 0)
    once = dict(pipeline_mode=pl.Buffered(1))
    kern = functools.partial(_ln_proj_kernel, nbp, d, dw, q_lora, kv_lora, hm)
    return pl.pallas_call(
        kern,
        grid=(nb, nst),
        in_specs=[
            pl.BlockSpec((1, tm, d), lambda bb, ss: (jnp.minimum(bb, nbp - 1), ss, 0)),
            pl.BlockSpec((1, tm, d), lambda bb, ss: (jnp.maximum(bb - nbp, 0), ss, 0)),
            pl.BlockSpec((1, 1, 6 * d), lambda bb, ss: (bb, 0, 0)),
            pl.BlockSpec(win.shape, const, **once),
            pl.BlockSpec(gq.shape, const),
            pl.BlockSpec(wq.shape, const, **once),
            pl.BlockSpec(gkv.shape, const),
            pl.BlockSpec(wk.shape, const, **once),
            pl.BlockSpec(wv.shape, const, **once),
            pl.BlockSpec((tm, MLA_QK_PAD), lambda bb, ss: (ss, 0)),
            pl.BlockSpec((tm, MLA_QK_PAD), lambda bb, ss: (ss, 0)),
            pl.BlockSpec((tm, MLA_QK_PAD), lambda bb, ss: (ss, 0)),
        ],
        out_specs=[
            pl.BlockSpec((tm, 3 * dw), row),
            pl.BlockSpec((tm, hm * MLA_QK_PAD), row),
            pl.BlockSpec((tm, hm * MLA_QK_PAD), row),
            pl.BlockSpec((tm, hm * MLA_V), row),
        ],
        out_shape=[
            jax.ShapeDtypeStruct((t, 3 * dw), BF16),
            jax.ShapeDtypeStruct((t, hm * MLA_QK_PAD), BF16),
            jax.ShapeDtypeStruct((t, hm * MLA_QK_PAD), BF16),
            jax.ShapeDtypeStruct((t, hm * MLA_V), BF16),
        ],
        compiler_params=_cparams(("arbitrary", "arbitrary")),
        name="ln_proj",
    )(xp, xs, mod3, win, gq, wq, gkv, wk, wv, rc, rsa, rsb)


def _diff_kernel(hd, tq, s, lam_init,
                 q_ref, k_ref, v_ref, lq1_ref, lk1_ref, lq2_ref, lk2_ref, g_ref, o_ref):
    lam = (jnp.exp(jnp.sum(lq1_ref[...] * lk1_ref[...], axis=-1, keepdims=True))
           - jnp.exp(jnp.sum(lq2_ref[...] * lk2_ref[...], axis=-1, keepdims=True)) + lam_init)
    q0 = pl.program_id(1) * tq
    rowp = lax.broadcasted_iota(I32, (tq, s), 0) + q0
    colp = lax.broadcasted_iota(I32, (tq, s), 1)
    dist = jnp.abs(rowp - colp).astype(F32)
    dv = 2 * DIFF_HEAD_DIM
    g = g_ref[...]
    for h in range(hd):
        bias = dist * (LOG2E * 2.0 ** (-8.0 * (h + 1) / hd))
        v = v_ref[:, h * dv:(h + 1) * dv]
        om = []
        for m in range(2):
            sl = slice(h * dv + m * DIFF_HEAD_DIM, h * dv + (m + 1) * DIFF_HEAD_DIM)
            om.append(_softmax_pv(_dot_nt(q_ref[:, sl], k_ref[:, sl]) - bias, v))
        o = om[0] - lam * om[1]
        o_ref[:, h * dv:(h + 1) * dv] = (_rms(o, g) * (1.0 - lam_init)).astype(BF16)


def _diff_attn(qkv, lq1, lk1, lq2, lk2, g, nb, s, dw, lam_init, tq):
    hd = dw // (2 * DIFF_HEAD_DIM)
    nq = s // tq
    vec = pl.BlockSpec((1, DIFF_HEAD_DIM), lambda b, i: (0, 0))
    kern = functools.partial(_diff_kernel, hd, tq, s, lam_init)
    return pl.pallas_call(
        kern,
        grid=(nb, nq),
        in_specs=[pl.BlockSpec((tq, dw), lambda b, i: (b * nq + i, 0)),
                  pl.BlockSpec((s, dw), lambda b, i: (b, 1)),
                  pl.BlockSpec((s, dw), lambda b, i: (b, 2)),
                  vec, vec, vec, vec,
                  pl.BlockSpec((1, 2 * DIFF_HEAD_DIM), lambda b, i: (0, 0))],
        out_specs=pl.BlockSpec((tq, dw), lambda b, i: (b * nq + i, 0)),
        out_shape=jax.ShapeDtypeStruct((nb * s, dw), BF16),
        compiler_params=_cparams(("arbitrary", "arbitrary")),
        name="diff_attn",
    )(qkv, qkv, qkv, lq1, lk1, lq2, lk2, g)


def _mla_kernel(hm, q_ref, k_ref, v_ref, o_ref):
    for h in range(hm):
        sl = slice(h * MLA_QK_PAD, (h + 1) * MLA_QK_PAD)
        o_ref[:, h * MLA_V:(h + 1) * MLA_V] = _softmax_pv(
            _dot_nt(q_ref[:, sl], k_ref[:, sl]), v_ref[:, h * MLA_V:(h + 1) * MLA_V]).astype(BF16)


def _mla_attn(qm, km, vm, nb, s, tq):
    hm = vm.shape[1] // MLA_V
    nq = s // tq
    return pl.pallas_call(
        functools.partial(_mla_kernel, hm),
        grid=(nb, nq),
        in_specs=[pl.BlockSpec((tq, hm * MLA_QK_PAD), lambda b, i: (b * nq + i, 0)),
                  pl.BlockSpec((s, hm * MLA_QK_PAD), lambda b, i: (b, 0)),
                  pl.BlockSpec((s, hm * MLA_V), lambda b, i: (b, 0))],
        out_specs=pl.BlockSpec((tq, hm * MLA_V), lambda b, i: (b * nq + i, 0)),
        out_shape=jax.ShapeDtypeStruct((nb * s, hm * MLA_V), BF16),
        compiler_params=_cparams(("arbitrary", "arbitrary")),
        name="mla_attn",
    )(qm, km, vm)


def _post_kernel(nbp, d, dw, ne, tm, alpha,
                 od_ref, om_ref, xp_ref, xs_ref, mod_ref, wo_ref, g1_ref, b1_ref, wr_ref, br_ref,
                 x1_ref, h2_ref, idx_ref, gate_ref, rank_ref, cnt_ref, run_ref):
    b = pl.program_id(0)
    first = jnp.logical_and(b == 0, pl.program_id(1) == 0)

    @pl.when(first)
    def _():
        run_ref[...] = jnp.zeros_like(run_ref)

    mix = _dot(od_ref[...], wo_ref[0:dw, :]) + _dot(om_ref[...], wo_ref[dw:2 * dw, :])
    x = jnp.where(b < nbp, xp_ref[0], xs_ref[0])
    mod = mod_ref[0]
    x1 = _ln(alpha * x + mod[:, 2 * d:3 * d] * mix) * g1_ref[...] + b1_ref[...]
    x1_ref[...] = x1
    h2 = _ln(x1) * (1.0 + mod[:, 4 * d:5 * d]) + mod[:, 3 * d:4 * d]
    h2_ref[...] = h2
    logits = _dot(h2.astype(BF16), wr_ref[...]) + br_ref[...]

    lane = lax.broadcasted_iota(I32, (tm, ne), 1).astype(F32)
    work = logits
    vals, idxs = [], []
    for _ in range(TOP_K):
        m = jnp.max(work, axis=-1, keepdims=True)
        i = jnp.min(jnp.where(work == m, lane, float(ne)), axis=-1, keepdims=True)
        vals.append(m)
        idxs.append(i)
        work = jnp.where(lane == i, -jnp.inf, work)
    es = [jnp.exp(v - vals[0]) for v in vals]
    inv = 1.0 / (es[0] + es[1] + es[2] + es[3])

    ohs = [jnp.where(lane == i, 1.0, 0.0) for i in idxs]
    oh = ohs[0] + ohs[1] + ohs[2] + ohs[3]
    tri = jnp.where(lax.broadcasted_iota(I32, (tm, tm), 0) > lax.broadcasted_iota(I32, (tm, tm), 1),
                    1.0, 0.0)
    before = _dot(tri.astype(BF16), oh.astype(BF16)) + run_ref[...]
    run_ref[...] = run_ref[...] + jnp.sum(oh, axis=0, keepdims=True)
    cnt_ref[...] = run_ref[...]

    lane_o = lax.broadcasted_iota(I32, (tm, LANES), 1)
    idx_o = jnp.zeros((tm, LANES), I32)
    gate_o = jnp.zeros((tm, LANES), F32)
    rank_o = jnp.zeros((tm, LANES), I32)
    for k in range(TOP_K):
        rk = jnp.sum(ohs[k] * before, axis=-1, keepdims=True).astype(I32)
        idx_o = jnp.where(lane_o == k, idxs[k].astype(I32), idx_o)
        gate_o = jnp.where(lane_o == k, es[k] * inv, gate_o)
        rank_o = jnp.where(lane_o == k, rk, rank_o)
    idx_ref[...] = idx_o
    gate_ref[...] = gate_o
    rank_ref[...] = rank_o


def _post(od, om, xp, xs, mod3, wo, g1, b1, wr, br, alpha, tm):
    nbp, s, d = xp.shape
    nb = nbp + xs.shape[0]
    dw = d // 2
    ne = wr.shape[1]
    nst = s // tm
    t = nb * s
    const = lambda bb, ss: (0, 0)
    row = lambda bb, ss: (bb * nst + ss, 0)
    kern = functools.partial(_post_kernel, nbp, d, dw, ne, tm, alpha)
    return pl.pallas_call(
        kern,
        grid=(nb, nst),
        in_specs=[
            pl.BlockSpec((tm, dw), row),
            pl.BlockSpec((tm, dw), row),
            pl.BlockSpec((1, tm, d), lambda bb, ss: (jnp.minimum(bb, nbp - 1), ss, 0)),
            pl.BlockSpec((1, tm, d), lambda bb, ss: (jnp.maximum(bb - nbp, 0), ss, 0)),
            pl.BlockSpec((1, 1, 6 * d), lambda bb, ss: (bb, 0, 0)),
            pl.BlockSpec(wo.shape, const, pipeline_mode=pl.Buffered(1)),
            pl.BlockSpec((1, d), const),
            pl.BlockSpec((1, d), const),
            pl.BlockSpec(wr.shape, const),
            pl.BlockSpec((1, ne), const),
        ],
        out_specs=[
            pl.BlockSpec((tm, d), row),
            pl.BlockSpec((tm, d), row),
            pl.BlockSpec((tm, LANES), row),
            pl.BlockSpec((tm, LANES), row),
            pl.BlockSpec((tm, LANES), row),
            pl.BlockSpec((1, ne), const),
        ],
        out_shape=[
            jax.ShapeDtypeStruct((t, d), F32),
            jax.ShapeDtypeStruct((t, d), F32),
            jax.ShapeDtypeStruct((t, LANES), I32),
            jax.ShapeDtypeStruct((t, LANES), F32),
            jax.ShapeDtypeStruct((t, LANES), I32),
            jax.ShapeDtypeStruct((1, ne), F32),
        ],
        scratch_shapes=[pltpu.VMEM((1, ne), F32)],
        compiler_params=_cparams(("arbitrary", "arbitrary")),
        name="post_mixer",
    )(od, om, xp, xs, mod3, wo, g1, b1, wr, br)


def _expert_kernel(tm, nblk, nf,
                   be_ref, nu_ref, tok0_ref, tokn_ref, src_ref, wg_ref, wu_ref, bg_ref, bu_ref, w2_ref,
                   b2_ref, o_ref, xg_ref, xb_ref, sem):
    i = pl.program_id(0)
    j = pl.program_id(1)
    n_used = nu_ref[0]
    used = i < n_used
    slot = i % 2
    rows_per_step = tm // nf

    def row_copy(tok_ref, r, s):
        return pltpu.make_async_copy(src_ref.at[pl.ds(tok_ref[r], 1)], xg_ref.at[s, pl.ds(r, 1)],
                                     sem.at[s])

    def slot_wait(s):
        pltpu.make_async_copy(src_ref.at[pl.ds(0, tm)], xg_ref.at[s], sem.at[s]).wait()

    @pl.when(jnp.logical_and(i == 0, j == 0))
    def _():
        def issue(r, carry):
            row_copy(tok0_ref, r, 0).start()
            return carry
        lax.fori_loop(0, tm, issue, 0, unroll=8)

    @pl.when(jnp.logical_and(j == 0, i <= n_used))
    def _():
        slot_wait(slot)
        xb_ref[...] = xg_ref[slot].astype(BF16)

    @pl.when(j == 0)
    def _():
        o_ref[...] = jnp.broadcast_to(b2_ref[0], o_ref.shape)

    @pl.when(used)
    def _():
        for r in range(rows_per_step):
            row_copy(tokn_ref, j * rows_per_step + r, 1 - slot).start()
        xb = xb_ref[...]
        hg = _dot_nt(xb, wg_ref[0]) + bg_ref[0]
        hu = _dot_nt(xb, wu_ref[0]) + bu_ref[0]
        glu = jnp.minimum(hg, SWIGLU_LIMIT)
        lin = jnp.clip(hu, -SWIGLU_LIMIT, SWIGLU_LIMIT)
        act = glu * jax.nn.sigmoid(SWIGLU_ALPHA * glu) * (lin + 1.0)
        o_ref[...] += _dot(act.astype(BF16), w2_ref[0])

    @pl.when(jnp.logical_and(jnp.logical_and(i == nblk - 1, j == nf - 1), used))
    def _():
        slot_wait(1 - slot)


def _experts(block_e, n_used, tok, src, wgt, wut, bg, bu, w2, b2, tm, tf):
    ne, f, d = wgt.shape
    p = tok.shape[0]
    nblk = p // tm
    nf = f // tf

    def wtile(i, j, be, nu):
        return (be[i], jnp.where(i < nu[0], j, nf - 1), 0)

    def btile(i, j, be, nu):
        return (be[i], 0, jnp.where(i < nu[0], j, nf - 1))

    grid_spec = pltpu.PrefetchScalarGridSpec(
        num_scalar_prefetch=2,
        grid=(nblk, nf),
        in_specs=[
            pl.BlockSpec((tm,), lambda i, j, be, nu: (0,), memory_space=pltpu.SMEM),
            pl.BlockSpec((tm,), lambda i, j, be, nu: (jnp.minimum(i + 1, nblk - 1),),
                         memory_space=pltpu.SMEM),
            pl.BlockSpec(memory_space=pl.ANY),
            pl.BlockSpec((1, tf, d), wtile),
            pl.BlockSpec((1, tf, d), wtile),
            pl.BlockSpec((1, 1, tf), btile),
            pl.BlockSpec((1, 1, tf), btile),
            pl.BlockSpec((1, tf, d), wtile),
            pl.BlockSpec((1, 1, d), lambda i, j, be, nu: (be[i], 0, 0)),
        ],
        out_specs=pl.BlockSpec((tm, d), lambda i, j, be, nu: (i, 0)),
        scratch_shapes=[pltpu.VMEM((2, tm, d), F32), pltpu.VMEM((tm, d), BF16),
                        pltpu.SemaphoreType.DMA((2,))],
    )
    return pl.pallas_call(
        functools.partial(_expert_kernel, tm, nblk, nf),
        grid_spec=grid_spec,
        out_shape=jax.ShapeDtypeStruct((p, d), F32),
        compiler_params=_cparams(("arbitrary", "arbitrary")),
        name="experts",
    )(block_e, n_used, tok, tok, src, wgt, wut, bg, bu, w2, b2)


def _split_w1_kernel(f, x_ref, g_ref, u_ref, t_ref):
    xt = x_ref[0].T
    for c in range(t_ref.shape[0]):
        sl = slice(c * LANES, (c + 1) * LANES)
        t_ref[c] = xt[:, sl]
        g_ref[0, :, sl] = t_ref[c, pl.ds(0, f, stride=2), :].astype(BF16)
        u_ref[0, :, sl] = t_ref[c, pl.ds(1, f, stride=2), :].astype(BF16)


def _split_w1(w1, td):
    ne, d, f2 = w1.shape
    f = f2 // 2
    td = min(td, d)
    out = jax.ShapeDtypeStruct((ne, f, d), BF16)
    return pl.pallas_call(
        functools.partial(_split_w1_kernel, f),
        grid=(ne, d // td),
        in_specs=[pl.BlockSpec((1, td, f2), lambda e, i: (e, i, 0))],
        out_specs=[pl.BlockSpec((1, f, td), lambda e, i: (e, 0, i)),
                   pl.BlockSpec((1, f, td), lambda e, i: (e, 0, i))],
        out_shape=[out, out],
        scratch_shapes=[pltpu.VMEM((td // LANES, f2, LANES), F32)],
        compiler_params=_cparams(("arbitrary", "arbitrary")),
        name="split_w1",
    )(w1)


def _final_kernel(d, tm, alpha,
                  dest_ref, x1_ref, mod_ref, gate_ref, src_ref, g2_ref, b2_ref, o_ref,
                  rows_ref, sem):
    def issue(r, carry):
        for k in range(TOP_K):
            pltpu.make_async_copy(src_ref.at[pl.ds(dest_ref[r * TOP_K + k], 1)],
                                  rows_ref.at[k, pl.ds(r, 1)], sem).start()
        return carry

    lax.fori_loop(0, tm, issue, 0, unroll=4)
    for k in range(TOP_K):
        pltpu.make_async_copy(src_ref.at[pl.ds(0, tm)], rows_ref.at[k], sem).wait()
    gate = gate_ref[...]
    y = gate[:, 0:1] * rows_ref[0]
    for k in range(1, TOP_K):
        y = y + gate[:, k:k + 1] * rows_ref[k]
    mod = mod_ref[0]
    z = alpha * x1_ref[...] + mod[:, 5 * d:6 * d] * y
    o_ref[0] = _ln(z) * g2_ref[...] + b2_ref[...]


def _final(dest_flat, x1, mod3, gate, eo, g2, b2, alpha, b0, nbg, s, tm):
    d = x1.shape[1]
    nst = s // tm
    off = b0 * nst
    kern = functools.partial(_final_kernel, d, tm, alpha)
    return pl.pallas_call(
        kern,
        grid=(nbg * nst,),
        in_specs=[
            pl.BlockSpec((tm * TOP_K,), lambda i: (i + off,), memory_space=pltpu.SMEM),
            pl.BlockSpec((tm, d), lambda i: (i + off, 0)),
            pl.BlockSpec((1, 1, 6 * d), lambda i: (b0 + i // nst, 0, 0)),
            pl.BlockSpec((tm, LANES), lambda i: (i + off, 0)),
            pl.BlockSpec(memory_space=pl.ANY),
            pl.BlockSpec((1, d), lambda i: (0, 0)),
            pl.BlockSpec((1, d), lambda i: (0, 0)),
        ],
        out_specs=pl.BlockSpec((1, tm, d), lambda i: (i // nst, i % nst, 0)),
        out_shape=jax.ShapeDtypeStruct((nbg, s, d), F32),
        scratch_shapes=[pltpu.VMEM((TOP_K, tm, d), F32), pltpu.SemaphoreType.DMA(())],
        compiler_params=_cparams(("arbitrary",)),
        name="final_combine",
    )(dest_flat, x1, mod3, gate, eo, g2, b2)


def _rope_tables(s):
    inv = 1.0 / (ROPE_THETA ** (jnp.arange(0, MLA_ROPE, 2, dtype=F32) / MLA_ROPE))
    ang = jnp.arange(s, dtype=F32)[:, None] * inv[None, :]
    cos, sin = jnp.cos(ang), jnp.sin(ang)
    one = jnp.ones((s, MLA_NOPE), F32)
    z32 = jnp.zeros((s, ROPE_HALF), F32)
    ztail = jnp.zeros((s, MLA_QK_PAD - MLA_NOPE - MLA_ROPE), F32)
    znope = jnp.zeros((s, MLA_NOPE), F32)
    c = jnp.concatenate([one, cos, cos, ztail], axis=1)
    sa = jnp.concatenate([znope, z32, sin, ztail], axis=1)
    sb = jnp.concatenate([znope, -sin, z32, ztail], axis=1)
    return c, sa, sb


def _pad_cols(w, width):
    return jnp.pad(w, ((0, 0), (0, width - w.shape[1])))


def _layer(xp, xs, mod3, lp, layer_idx, alpha):
    nbp, s, d = xp.shape
    nb = nbp + xs.shape[0]
    dw = d // 2
    hm = (d - dw) // MLA_V
    ne = lp["w_router"].shape[1]
    f = lp["w_mlp2"].shape[1]
    t = nb * s
    tm = min(256, s)
    tq = min(512, s)

    w_in = lp["w_in"]
    q_lora = lp["mla_q_norm_g"].shape[0]
    kv_lora = lp["mla_kv_norm_g"].shape[0]
    o3, o4, o5 = 3 * dw, 3 * dw + q_lora, 3 * dw + q_lora + kv_lora
    kpe_w = jnp.concatenate([jnp.zeros((d, MLA_NOPE), F32), w_in[:, o5:],
                             jnp.zeros((d, MLA_QK_PAD - MLA_NOPE - MLA_ROPE), F32)], axis=1)
    dq_w = w_in[:, :dw] * (DIFF_HEAD_DIM ** -0.5 * LOG2E)
    win = jnp.concatenate([dq_w, w_in[:, dw:o5], kpe_w], axis=1).astype(BF16)
    wq = lp["mla_w_q_up"] * ((MLA_NOPE + MLA_ROPE) ** -0.5 * LOG2E)
    wq = wq.reshape(q_lora, hm, MLA_NOPE + MLA_ROPE)
    wq = jnp.pad(wq, ((0, 0), (0, 0), (0, MLA_QK_PAD - MLA_NOPE - MLA_ROPE)))
    wq = wq.reshape(q_lora, hm * MLA_QK_PAD).astype(BF16)
    wkv = lp["mla_w_kv_up"].reshape(kv_lora, hm, MLA_NOPE + MLA_V)
    wk = jnp.pad(wkv[:, :, :MLA_NOPE], ((0, 0), (0, 0), (0, MLA_QK_PAD - MLA_NOPE)))
    wk = wk.reshape(kv_lora, hm * MLA_QK_PAD).astype(BF16)
    wv = wkv[:, :, MLA_NOPE:].reshape(kv_lora, hm * MLA_V).astype(BF16)
    rc, rsa, rsb = _rope_tables(s)

    qkv, qm, km, vm = _ln_proj(xp, xs, mod3, win, lp["mla_q_norm_g"].reshape(1, -1), wq,
                               lp["mla_kv_norm_g"].reshape(1, -1), wk, wv, rc, rsa, rsb, tm)

    lam_init = 0.8 - 0.6 * math.exp(-0.3 * layer_idx)
    od = _diff_attn(qkv, lp["diff_lambda_q1"].reshape(1, -1), lp["diff_lambda_k1"].reshape(1, -1),
                    lp["diff_lambda_q2"].reshape(1, -1), lp["diff_lambda_k2"].reshape(1, -1),
                    lp["diff_subln_g"].reshape(1, -1), nb, s, dw, lam_init, tq)
    om = _mla_attn(qm, km, vm, nb, s, tq)

    x1, h2, idx, gate, rank, cnt = _post(
        od, om, xp, xs, mod3, lp["w_out"].astype(BF16), lp["ln1_g"].reshape(1, -1),
        lp["ln1_b"].reshape(1, -1), lp["w_router"].astype(BF16), lp["b_router"].reshape(1, -1),
        alpha, tm)

    te = min(512, t * TOP_K // ne)
    nblk = (t * TOP_K) // te + ne
    p = nblk * te
    counts = cnt[0].astype(I32)
    padded = (counts + te - 1) // te * te
    pends = jnp.cumsum(padded)
    pstarts = pends - padded
    idx4 = idx[:, :TOP_K]
    onehot = idx4[:, :, None] == jnp.arange(ne, dtype=I32)[None, None, :]
    dest = jnp.sum(jnp.where(onehot, pstarts[None, None, :], 0), axis=-1) + rank[:, :TOP_K]
    dest_flat = dest.reshape(-1).astype(I32)
    buf_tok = jnp.zeros((p,), I32).at[dest_flat].set(jnp.arange(t * TOP_K, dtype=I32) // TOP_K)
    n_used = (pends[-1] // te).astype(I32)
    blk = jnp.minimum(jnp.arange(nblk, dtype=I32), n_used - 1)
    block_e = jnp.minimum(jnp.sum((pends[None, :] <= (blk * te)[:, None]).astype(I32), axis=1), ne - 1)

    b1 = lp["b_mlp1"]
    wgt, wut = _split_w1(lp["w_mlp1"], 256)
    bg = b1[:, 0::2].reshape(ne, 1, f)
    bu = b1[:, 1::2].reshape(ne, 1, f)
    eo = _experts(block_e, n_used.reshape(1), buf_tok, h2, wgt, wut, bg, bu,
                  lp["w_mlp2"].astype(BF16), lp["b_mlp2"].reshape(ne, 1, d), te, min(512, f))

    g2 = lp["ln2_g"].reshape(1, -1)
    b2 = lp["ln2_b"].reshape(1, -1)
    yp = _final(dest_flat, x1, mod3, gate, eo, g2, b2, alpha, 0, nbp, s, tm)
    ys = _final(dest_flat, x1, mod3, gate, eo, g2, b2, alpha, nbp, nb - nbp, s, tm)
    return yp, ys


def kernel(x_prompt, x_sample, c_prompt, c_sample, w_ada, b_ada, w_in, diff_lambda_q1, diff_lambda_k1, diff_lambda_q2, diff_lambda_k2, diff_subln_g, mla_q_norm_g, mla_w_q_up, mla_kv_norm_g, mla_w_kv_up, w_out, ln1_g, ln1_b, w_router, b_router, w_mlp1, b_mlp1, w_mlp2, b_mlp2, ln2_g, ln2_b):
    weights = dict(
        w_ada=w_ada, b_ada=b_ada, w_in=w_in, diff_lambda_q1=diff_lambda_q1,
        diff_lambda_k1=diff_lambda_k1, diff_lambda_q2=diff_lambda_q2, diff_lambda_k2=diff_lambda_k2,
        diff_subln_g=diff_subln_g, mla_q_norm_g=mla_q_norm_g, mla_w_q_up=mla_w_q_up,
        mla_kv_norm_g=mla_kv_norm_g, mla_w_kv_up=mla_w_kv_up, w_out=w_out, ln1_g=ln1_g, ln1_b=ln1_b,
        w_router=w_router, b_router=b_router, w_mlp1=w_mlp1, b_mlp1=b_mlp1, w_mlp2=w_mlp2,
        b_mlp2=b_mlp2, ln2_g=ln2_g, ln2_b=ln2_b)
    depth = w_ada.shape[0]
    assert x_prompt.shape[1:] == x_sample.shape[1:], "the two request groups are batched together"
    nbp, nbs = x_prompt.shape[0], x_sample.shape[0]
    nb = nbp + nbs
    d = x_prompt.shape[2]
    alpha = (2.0 * depth) ** 0.25
    bpad = -(-nb // 8) * 8
    c_pad = jnp.pad(jnp.concatenate([c_prompt, c_sample], axis=0), ((0, bpad - nb), (0, 0)))
    xp, xs = x_prompt, x_sample
    for i in range(depth):
        lp = {name: arr[i] for name, arr in weights.items()}
        mod3 = _ada(c_pad, lp["w_ada"], lp["b_ada"]).reshape(bpad, 1, 6 * d)
        xp, xs = _layer(xp, xs, mod3, lp, i, alpha)
    return (xp, xs)
```

```python
import functools
import math

import jax
import jax.numpy as jnp
from jax import lax
from jax.experimental import pallas as pl
from jax.experimental.pallas import tpu as pltpu

F32 = jnp.float32
BF16 = jnp.bfloat16
I32 = jnp.int32
U32 = jnp.uint32

DIFF_HEAD_DIM = 128
MLA_NOPE = 128
MLA_ROPE = 64
MLA_V = 128
ROPE_THETA = 10000.0
TOP_K = 4
SWIGLU_LIMIT = 7.0
SWIGLU_ALPHA = 1.702
LN_EPS = 1e-5
RMS_EPS = 1e-6
LOG2E = math.log2(math.e)

LANES = 128
MLA_QK_PAD = 2 * LANES
ROPE_HALF = MLA_ROPE // 2
VMEM_LIMIT = 56 * 1024 * 1024


def _cparams(sem, vmem=VMEM_LIMIT):
    return pltpu.CompilerParams(dimension_semantics=sem, vmem_limit_bytes=vmem)


def _ln(x):
    mu = jnp.mean(x, axis=-1, keepdims=True)
    xc = x - mu
    var = jnp.mean(xc * xc, axis=-1, keepdims=True)
    return xc * lax.rsqrt(var + LN_EPS)


def _rms(x, g):
    return x * lax.rsqrt(jnp.mean(x * x, axis=-1, keepdims=True) + RMS_EPS) * g


def _dot(a, b):
    return jnp.dot(a, b, preferred_element_type=F32)


def _dot_nt(a, b):
    return lax.dot_general(a, b, (((1,), (1,)), ((), ())), preferred_element_type=F32)


def _pack_bf16_pairs(x):
    half = x.shape[1] // 2
    xr = x.astype(jnp.bfloat16).astype(F32)
    hi = lax.bitcast_convert_type(xr[:, :half], U32)
    lo = lax.bitcast_convert_type(xr[:, half:], U32)
    return hi | (lo >> 16)


def _unpack_bf16_pairs(p):
    hi = lax.bitcast_convert_type(p & jnp.uint32(0xFFFF0000), F32)
    lo = lax.bitcast_convert_type(p << 16, F32)
    return hi, lo


def _softmax_pv(s2, v):
    m = jnp.max(s2, axis=-1, keepdims=True)
    e = jnp.exp2(s2 - m)
    inv = 1.0 / jnp.sum(e, axis=-1, keepdims=True)
    return _dot(e.astype(BF16), v) * inv


def _ada_kernel(c_ref, w_ref, b_ref, o_ref):
    c = c_ref[...]
    s = c * jax.nn.sigmoid(c)
    o_ref[...] = _dot(s.astype(BF16), w_ref[...].astype(BF16)) + b_ref[...]


def _ada(c_pad, w_ada, b_ada, tn=1024):
    bp, d = c_pad.shape
    n = w_ada.shape[1]
    tn = min(tn, n)
    return pl.pallas_call(
        _ada_kernel,
        grid=(n // tn,),
        in_specs=[pl.BlockSpec((bp, d), lambda j: (0, 0)),
                  pl.BlockSpec((d, tn), lambda j: (0, j)),
                  pl.BlockSpec((1, tn), lambda j: (0, j))],
        out_specs=pl.BlockSpec((bp, tn), lambda j: (0, j)),
        out_shape=jax.ShapeDtypeStruct((bp, n), F32),
        compiler_params=_cparams(("arbitrary",)),
        name="ada",
    )(c_pad, w_ada, b_ada.reshape(1, n))


def _rope_pad(x, c, sa, sb):
    return (x * c + pltpu.roll(x, ROPE_HALF, 1) * sa
            + pltpu.roll(x, MLA_QK_PAD - ROPE_HALF, 1) * sb)


def _ln_proj_kernel(nbp, d, dw, q_lora, kv_lora, hm,
                    xp_ref, xs_ref, mod_ref, win_ref, gq_ref, wq_ref, gkv_ref, wk_ref, wv_ref,
                    c_ref, sa_ref, sb_ref,
                    qkv_ref, qm_ref, km_ref, vm_ref):
    b = pl.program_id(0)
    x = jnp.where(b < nbp, xp_ref[0], xs_ref[0])
    mod = mod_ref[0]
    h = _ln(x) * (1.0 + mod[:, d:2 * d]) + mod[:, 0:d]
    hb = h.astype(BF16)
    o3 = 3 * dw
    qkv_ref[...] = _dot(hb, win_ref[:, 0:o3]).astype(BF16)
    cq = _dot(hb, win_ref[:, o3:o3 + q_lora])
    ckv = _dot(hb, win_ref[:, o3 + q_lora:o3 + q_lora + kv_lora])
    kpe = _dot(hb, win_ref[:, o3 + q_lora + kv_lora:o3 + q_lora + kv_lora + MLA_QK_PAD])
    c, sa, sb = c_ref[...], sa_ref[...], sb_ref[...]
    q = _dot(_rms(cq, gq_ref[...]).astype(BF16), wq_ref[...])
    ckvn = _rms(ckv, gkv_ref[...]).astype(BF16)
    kk = _dot(ckvn, wk_ref[...])
    vm_ref[...] = _dot(ckvn, wv_ref[...]).astype(BF16)
    kp = _rope_pad(kpe, c, sa, sb)
    for hh in range(hm):
        sl = slice(hh * MLA_QK_PAD, (hh + 1) * MLA_QK_PAD)
        qm_ref[:, sl] = _rope_pad(q[:, sl], c, sa, sb).astype(BF16)
        km_ref[:, sl] = (kk[:, sl] + kp).astype(BF16)


def _ln_proj(xp, xs, mod3, win, gq, wq, gkv, wk, wv, rc, rsa, rsb, tm):
    nbp, s, d = xp.shape
    nbs = xs.shape[0]
    nb = nbp + nbs
    dw = d // 2
    q_lora, kv_lora = gq.shape[1], gkv.shape[1]
    hm = wv.shape[1] // MLA_V
    nst = s // tm
    t = nb * s
    const = lambda bb, ss: (0, 0)
    row = lambda bb, ss: (bb * nst + ss, 0)
    once = dict(pipeline_mode=pl.Buffered(1))
    kern = functools.partial(_ln_proj_kernel, nbp, d, dw, q_lora, kv_lora, hm)
    return pl.pallas_call(
        kern,
        grid=(nb, nst),
        in_specs=[
            pl.BlockSpec((1, tm, d), lambda bb, ss: (jnp.minimum(bb, nbp - 1), ss, 0)),
            pl.BlockSpec((1, tm, d), lambda bb, ss: (jnp.maximum(bb - nbp, 0), ss, 0)),
            pl.BlockSpec((1, 1, 6 * d), lambda bb, ss: (bb, 0, 0)),
            pl.BlockSpec(win.shape, const, **once),
            pl.BlockSpec(gq.shape, const),
            pl.BlockSpec(wq.shape, const, **once),
            pl.BlockSpec(gkv.shape, const),
            pl.BlockSpec(wk.shape, const, **once),
            pl.BlockSpec(wv.shape, const, **once),
            pl.BlockSpec((tm, MLA_QK_PAD), lambda bb, ss: (ss, 0)),
            pl.BlockSpec((tm, MLA_QK_PAD), lambda bb, ss: (ss, 0)),
            pl.BlockSpec((tm, MLA_QK_PAD), lambda bb, ss: (ss, 0)),
        ],
        out_specs=[
            pl.BlockSpec((tm, 3 * dw), row),
            pl.BlockSpec((tm, hm * MLA_QK_PAD), row),
            pl.BlockSpec((tm, hm * MLA_QK_PAD), row),
            pl.BlockSpec((tm, hm * MLA_V), row),
        ],
        out_shape=[
            jax.ShapeDtypeStruct((t, 3 * dw), BF16),
            jax.ShapeDtypeStruct((t, hm * MLA_QK_PAD), BF16),
            jax.ShapeDtypeStruct((t, hm * MLA_QK_PAD), BF16),
            jax.ShapeDtypeStruct((t, hm * MLA_V), BF16),
        ],
        compiler_params=_cparams(("arbitrary", "arbitrary")),
        name="ln_proj",
    )(xp, xs, mod3, win, gq, wq, gkv, wk, wv, rc, rsa, rsb)


def _diff_kernel(hd, tq, s, lam_init,
                 q_ref, k_ref, v_ref, lq1_ref, lk1_ref, lq2_ref, lk2_ref, g_ref, o_ref):
    lam = (jnp.exp(jnp.sum(lq1_ref[...] * lk1_ref[...], axis=-1, keepdims=True))
           - jnp.exp(jnp.sum(lq2_ref[...] * lk2_ref[...], axis=-1, keepdims=True)) + lam_init)
    q0 = pl.program_id(1) * tq
    rowp = lax.broadcasted_iota(I32, (tq, s), 0) + q0
    colp = lax.broadcasted_iota(I32, (tq, s), 1)
    dist = jnp.abs(rowp - colp).astype(F32)
    dv = 2 * DIFF_HEAD_DIM
    g = g_ref[...]
    for h in range(hd):
        bias = dist * (LOG2E * 2.0 ** (-8.0 * (h + 1) / hd))
        v = v_ref[:, h * dv:(h + 1) * dv]
        om = []
        for m in range(2):
            sl = slice(h * dv + m * DIFF_HEAD_DIM, h * dv + (m + 1) * DIFF_HEAD_DIM)
            om.append(_softmax_pv(_dot_nt(q_ref[:, sl], k_ref[:, sl]) - bias, v))
        o = om[0] - lam * om[1]
        o_ref[:, h * dv:(h + 1) * dv] = (_rms(o, g) * (1.0 - lam_init)).astype(BF16)


def _diff_attn(qkv, lq1, lk1, lq2, lk2, g, nb, s, dw, lam_init, tq):
    hd = dw // (2 * DIFF_HEAD_DIM)
    nq = s // tq
    vec = pl.BlockSpec((1, DIFF_HEAD_DIM), lambda b, i: (0, 0))
    kern = functools.partial(_diff_kernel, hd, tq, s, lam_init)
    return pl.pallas_call(
        kern,
        grid=(nb, nq),
        in_specs=[pl.BlockSpec((tq, dw), lambda b, i: (b * nq + i, 0)),
                  pl.BlockSpec((s, dw), lambda b, i: (b, 1)),
                  pl.BlockSpec((s, dw), lambda b, i: (b, 2)),
                  vec, vec, vec, vec,
                  pl.BlockSpec((1, 2 * DIFF_HEAD_DIM), lambda b, i: (0, 0))],
        out_specs=pl.BlockSpec((tq, dw), lambda b, i: (b * nq + i, 0)),
        out_shape=jax.ShapeDtypeStruct((nb * s, dw), BF16),
        compiler_params=_cparams(("arbitrary", "arbitrary")),
        name="diff_attn",
    )(qkv, qkv, qkv, lq1, lk1, lq2, lk2, g)


def _mla_kernel(hm, q_ref, k_ref, v_ref, o_ref):
    for h in range(hm):
        sl = slice(h * MLA_QK_PAD, (h + 1) * MLA_QK_PAD)
        o_ref[:, h * MLA_V:(h + 1) * MLA_V] = _softmax_pv(
            _dot_nt(q_ref[:, sl], k_ref[:, sl]), v_ref[:, h * MLA_V:(h + 1) * MLA_V]).astype(BF16)


def _mla_attn(qm, km, vm, nb, s, tq):
    hm = vm.shape[1] // MLA_V
    nq = s // tq
    return pl.pallas_call(
        functools.partial(_mla_kernel, hm),
        grid=(nb, nq),
        in_specs=[pl.BlockSpec((tq, hm * MLA_QK_PAD), lambda b, i: (b * nq + i, 0)),
                  pl.BlockSpec((s, hm * MLA_QK_PAD), lambda b, i: (b, 0)),
                  pl.BlockSpec((s, hm * MLA_V), lambda b, i: (b, 0))],
        out_specs=pl.BlockSpec((tq, hm * MLA_V), lambda b, i: (b * nq + i, 0)),
        out_shape=jax.ShapeDtypeStruct((nb * s, hm * MLA_V), BF16),
        compiler_params=_cparams(("arbitrary", "arbitrary")),
        name="mla_attn",
    )(qm, km, vm)


def _post_kernel(nbp, d, dw, ne, tm, alpha,
                 od_ref, om_ref, xp_ref, xs_ref, mod_ref, wo_ref, g1_ref, b1_ref, wr_ref, br_ref,
                 x1_ref, h2_ref, idx_ref, gate_ref, rank_ref, cnt_ref, run_ref):
    b = pl.program_id(0)
    first = jnp.logical_and(b == 0, pl.program_id(1) == 0)

    @pl.when(first)
    def _():
        run_ref[...] = jnp.zeros_like(run_ref)

    mix = _dot(od_ref[...], wo_ref[0:dw, :]) + _dot(om_ref[...], wo_ref[dw:2 * dw, :])
    x = jnp.where(b < nbp, xp_ref[0], xs_ref[0])
    mod = mod_ref[0]
    x1 = _ln(alpha * x + mod[:, 2 * d:3 * d] * mix) * g1_ref[...] + b1_ref[...]
    x1_ref[...] = x1
    h2 = _ln(x1) * (1.0 + mod[:, 4 * d:5 * d]) + mod[:, 3 * d:4 * d]
    h2_ref[...] = _pack_bf16_pairs(h2)
    logits = _dot(h2.astype(BF16), wr_ref[...]) + br_ref[...]

    lane = lax.broadcasted_iota(I32, (tm, ne), 1).astype(F32)
    work = logits
    vals, idxs = [], []
    for _ in range(TOP_K):
        m = jnp.max(work, axis=-1, keepdims=True)
        i = jnp.min(jnp.where(work == m, lane, float(ne)), axis=-1, keepdims=True)
        vals.append(m)
        idxs.append(i)
        work = jnp.where(lane == i, -jnp.inf, work)
    es = [jnp.exp(v - vals[0]) for v in vals]
    inv = 1.0 / (es[0] + es[1] + es[2] + es[3])

    ohs = [jnp.where(lane == i, 1.0, 0.0) for i in idxs]
    oh = ohs[0] + ohs[1] + ohs[2] + ohs[3]
    tri = jnp.where(lax.broadcasted_iota(I32, (tm, tm), 0) > lax.broadcasted_iota(I32, (tm, tm), 1),
                    1.0, 0.0)
    before = _dot(tri.astype(BF16), oh.astype(BF16)) + run_ref[...]
    run_ref[...] = run_ref[...] + jnp.sum(oh, axis=0, keepdims=True)
    cnt_ref[...] = run_ref[...]

    lane_o = lax.broadcasted_iota(I32, (tm, LANES), 1)
    idx_o = jnp.zeros((tm, LANES), I32)
    gate_o = jnp.zeros((tm, LANES), F32)
    rank_o = jnp.zeros((tm, LANES), I32)
    for k in range(TOP_K):
        rk = jnp.sum(ohs[k] * before, axis=-1, keepdims=True).astype(I32)
        idx_o = jnp.where(lane_o == k, idxs[k].astype(I32), idx_o)
        gate_o = jnp.where(lane_o == k, es[k] * inv, gate_o)
        rank_o = jnp.where(lane_o == k, rk, rank_o)
    idx_ref[...] = idx_o
    gate_ref[...] = gate_o
    rank_ref[...] = rank_o


def _post(od, om, xp, xs, mod3, wo, g1, b1, wr, br, alpha, tm):
    nbp, s, d = xp.shape
    nb = nbp + xs.shape[0]
    dw = d // 2
    ne = wr.shape[1]
    nst = s // tm
    t = nb * s
    const = lambda bb, ss: (0, 0)
    row = lambda bb, ss: (bb * nst + ss, 0)
    kern = functools.partial(_post_kernel, nbp, d, dw, ne, tm, alpha)
    return pl.pallas_call(
        kern,
        grid=(nb, nst),
        in_specs=[
            pl.BlockSpec((tm, dw), row),
            pl.BlockSpec((tm, dw), row),
            pl.BlockSpec((1, tm, d), lambda bb, ss: (jnp.minimum(bb, nbp - 1), ss, 0)),
            pl.BlockSpec((1, tm, d), lambda bb, ss: (jnp.maximum(bb - nbp, 0), ss, 0)),
            pl.BlockSpec((1, 1, 6 * d), lambda bb, ss: (bb, 0, 0)),
            pl.BlockSpec(wo.shape, const, pipeline_mode=pl.Buffered(1)),
            pl.BlockSpec((1, d), const),
            pl.BlockSpec((1, d), const),
            pl.BlockSpec(wr.shape, const),
            pl.BlockSpec((1, ne), const),
        ],
        out_specs=[
            pl.BlockSpec((tm, d), row),
            pl.BlockSpec((tm, d // 2), row),
            pl.BlockSpec((tm, LANES), row),
            pl.BlockSpec((tm, LANES), row),
            pl.BlockSpec((tm, LANES), row),
            pl.BlockSpec((1, ne), const),
        ],
        out_shape=[
            jax.ShapeDtypeStruct((t, d), F32),
            jax.ShapeDtypeStruct((t, d // 2), U32),
            jax.ShapeDtypeStruct((t, LANES), I32),
            jax.ShapeDtypeStruct((t, LANES), F32),
            jax.ShapeDtypeStruct((t, LANES), I32),
            jax.ShapeDtypeStruct((1, ne), F32),
        ],
        scratch_shapes=[pltpu.VMEM((1, ne), F32)],
        compiler_params=_cparams(("arbitrary", "arbitrary")),
        name="post_mixer",
    )(od, om, xp, xs, mod3, wo, g1, b1, wr, br)


def _expert_kernel(tm, nblk, nf,
                   be_ref, nu_ref, tok0_ref, tokn_ref, src_ref, wg_ref, wu_ref, bg_ref, bu_ref, w2_ref,
                   b2_ref, o_ref, xg_ref, xb_ref, sem):
    i = pl.program_id(0)
    j = pl.program_id(1)
    n_used = nu_ref[0]
    used = i < n_used
    slot = i % 2
    rows_per_step = tm // nf

    def row_copy(tok_ref, r, s):
        return pltpu.make_async_copy(src_ref.at[pl.ds(tok_ref[r], 1)], xg_ref.at[s, pl.ds(r, 1)],
                                     sem.at[s])

    def slot_wait(s):
        pltpu.make_async_copy(src_ref.at[pl.ds(0, tm)], xg_ref.at[s], sem.at[s]).wait()

    @pl.when(jnp.logical_and(i == 0, j == 0))
    def _():
        def issue(r, carry):
            row_copy(tok0_ref, r, 0).start()
            return carry
        lax.fori_loop(0, tm, issue, 0, unroll=8)

    @pl.when(jnp.logical_and(j == 0, i <= n_used))
    def _():
        slot_wait(slot)
        hi, lo = _unpack_bf16_pairs(xg_ref[slot])
        half = hi.shape[1]
        xb_ref[:, :half] = hi.astype(BF16)
        xb_ref[:, half:] = lo.astype(BF16)

    @pl.when(j == 0)
    def _():
        o_ref[...] = jnp.broadcast_to(b2_ref[0], o_ref.shape)

    @pl.when(used)
    def _():
        for r in range(rows_per_step):
            row_copy(tokn_ref, j * rows_per_step + r, 1 - slot).start()
        xb = xb_ref[...]
        hg = _dot_nt(xb, wg_ref[0]) + bg_ref[0]
        hu = _dot_nt(xb, wu_ref[0]) + bu_ref[0]
        glu = jnp.minimum(hg, SWIGLU_LIMIT)
        lin = jnp.clip(hu, -SWIGLU_LIMIT, SWIGLU_LIMIT)
        act = glu * jax.nn.sigmoid(SWIGLU_ALPHA * glu) * (lin + 1.0)
        o_ref[...] += _dot(act.astype(BF16), w2_ref[0])

    @pl.when(jnp.logical_and(jnp.logical_and(i == nblk - 1, j == nf - 1), used))
    def _():
        slot_wait(1 - slot)


def _experts(block_e, n_used, tok, src, wgt, wut, bg, bu, w2, b2, tm, tf):
    ne, f, d = wgt.shape
    p = tok.shape[0]
    nblk = p // tm
    nf = f // tf

    def wtile(i, j, be, nu):
        return (be[i], jnp.where(i < nu[0], j, nf - 1), 0)

    def btile(i, j, be, nu):
        return (be[i], 0, jnp.where(i < nu[0], j, nf - 1))

    grid_spec = pltpu.PrefetchScalarGridSpec(
        num_scalar_prefetch=2,
        grid=(nblk, nf),
        in_specs=[
            pl.BlockSpec((tm,), lambda i, j, be, nu: (0,), memory_space=pltpu.SMEM),
            pl.BlockSpec((tm,), lambda i, j, be, nu: (jnp.minimum(i + 1, nblk - 1),),
                         memory_space=pltpu.SMEM),
            pl.BlockSpec(memory_space=pl.ANY),
            pl.BlockSpec((1, tf, d), wtile),
            pl.BlockSpec((1, tf, d), wtile),
            pl.BlockSpec((1, 1, tf), btile),
            pl.BlockSpec((1, 1, tf), btile),
            pl.BlockSpec((1, tf, d), wtile),
            pl.BlockSpec((1, 1, d), lambda i, j, be, nu: (be[i], 0, 0)),
        ],
        out_specs=pl.BlockSpec((tm, d), lambda i, j, be, nu: (i, 0)),
        scratch_shapes=[pltpu.VMEM((2, tm, d // 2), U32), pltpu.VMEM((tm, d), BF16),
                        pltpu.SemaphoreType.DMA((2,))],
    )
    return pl.pallas_call(
        functools.partial(_expert_kernel, tm, nblk, nf),
        grid_spec=grid_spec,
        out_shape=jax.ShapeDtypeStruct((p, d), F32),
        compiler_params=_cparams(("arbitrary", "arbitrary")),
        name="experts",
    )(block_e, n_used, tok, tok, src, wgt, wut, bg, bu, w2, b2)


def _split_w1_kernel(f, x_ref, g_ref, u_ref, t_ref):
    xt = x_ref[0].T
    for c in range(t_ref.shape[0]):
        sl = slice(c * LANES, (c + 1) * LANES)
        t_ref[c] = xt[:, sl]
        g_ref[0, :, sl] = t_ref[c, pl.ds(0, f, stride=2), :].astype(BF16)
        u_ref[0, :, sl] = t_ref[c, pl.ds(1, f, stride=2), :].astype(BF16)


def _split_w1(w1, td):
    ne, d, f2 = w1.shape
    f = f2 // 2
    td = min(td, d)
    out = jax.ShapeDtypeStruct((ne, f, d), BF16)
    return pl.pallas_call(
        functools.partial(_split_w1_kernel, f),
        grid=(ne, d // td),
        in_specs=[pl.BlockSpec((1, td, f2), lambda e, i: (e, i, 0))],
        out_specs=[pl.BlockSpec((1, f, td), lambda e, i: (e, 0, i)),
                   pl.BlockSpec((1, f, td), lambda e, i: (e, 0, i))],
        out_shape=[out, out],
        scratch_shapes=[pltpu.VMEM((td // LANES, f2, LANES), F32)],
        compiler_params=_cparams(("arbitrary", "arbitrary")),
        name="split_w1",
    )(w1)


def _final_kernel(d, tm, alpha,
                  dest_ref, x1_ref, mod_ref, gate_ref, src_ref, g2_ref, b2_ref, o_ref,
                  rows_ref, sem):
    def issue(r, carry):
        for k in range(TOP_K):
            pltpu.make_async_copy(src_ref.at[pl.ds(dest_ref[r * TOP_K + k], 1)],
                                  rows_ref.at[k, pl.ds(r, 1)], sem).start()
        return carry

    lax.fori_loop(0, tm, issue, 0, unroll=4)
    for k in range(TOP_K):
        pltpu.make_async_copy(src_ref.at[pl.ds(0, tm)], rows_ref.at[k], sem).wait()
    gate = gate_ref[...]
    y = gate[:, 0:1] * rows_ref[0]
    for k in range(1, TOP_K):
        y = y + gate[:, k:k + 1] * rows_ref[k]
    mod = mod_ref[0]
    z = alpha * x1_ref[...] + mod[:, 5 * d:6 * d] * y
    o_ref[0] = _ln(z) * g2_ref[...] + b2_ref[...]


def _final(dest_flat, x1, mod3, gate, eo, g2, b2, alpha, b0, nbg, s, tm):
    d = x1.shape[1]
    nst = s // tm
    off = b0 * nst
    kern = functools.partial(_final_kernel, d, tm, alpha)
    return pl.pallas_call(
        kern,
        grid=(nbg * nst,),
        in_specs=[
            pl.BlockSpec((tm * TOP_K,), lambda i: (i + off,), memory_space=pltpu.SMEM),
            pl.BlockSpec((tm, d), lambda i: (i + off, 0)),
            pl.BlockSpec((1, 1, 6 * d), lambda i: (b0 + i // nst, 0, 0)),
            pl.BlockSpec((tm, LANES), lambda i: (i + off, 0)),
            pl.BlockSpec(memory_space=pl.ANY),
            pl.BlockSpec((1, d), lambda i: (0, 0)),
            pl.BlockSpec((1, d), lambda i: (0, 0)),
        ],
        out_specs=pl.BlockSpec((1, tm, d), lambda i: (i // nst, i % nst, 0)),
        out_shape=jax.ShapeDtypeStruct((nbg, s, d), F32),
        scratch_shapes=[pltpu.VMEM((TOP_K, tm, d), F32), pltpu.SemaphoreType.DMA(())],
        compiler_params=_cparams(("arbitrary",)),
        name="final_combine",
    )(dest_flat, x1, mod3, gate, eo, g2, b2)


def _rope_tables(s):
    inv = 1.0 / (ROPE_THETA ** (jnp.arange(0, MLA_ROPE, 2, dtype=F32) / MLA_ROPE))
    ang = jnp.arange(s, dtype=F32)[:, None] * inv[None, :]
    cos, sin = jnp.cos(ang), jnp.sin(ang)
    one = jnp.ones((s, MLA_NOPE), F32)
    z32 = jnp.zeros((s, ROPE_HALF), F32)
    ztail = jnp.zeros((s, MLA_QK_PAD - MLA_NOPE - MLA_ROPE), F32)
    znope = jnp.zeros((s, MLA_NOPE), F32)
    c = jnp.concatenate([one, cos, cos, ztail], axis=1)
    sa = jnp.concatenate([znope, z32, sin, ztail], axis=1)
    sb = jnp.concatenate([znope, -sin, z32, ztail], axis=1)
    return c, sa, sb


def _pad_cols(w, width):
    return jnp.pad(w, ((0, 0), (0, width - w.shape[1])))


def _layer(xp, xs, mod3, lp, layer_idx, alpha):
    nbp, s, d = xp.shape
    nb = nbp + xs.shape[0]
    dw = d // 2
    hm = (d - dw) // MLA_V
    ne = lp["w_router"].shape[1]
    f = lp["w_mlp2"].shape[1]
    t = nb * s
    tm = min(256, s)
    tq = min(512, s)

    w_in = lp["w_in"]
    q_lora = lp["mla_q_norm_g"].shape[0]
    kv_lora = lp["mla_kv_norm_g"].shape[0]
    o3, o4, o5 = 3 * dw, 3 * dw + q_lora, 3 * dw + q_lora + kv_lora
    kpe_w = jnp.concatenate([jnp.zeros((d, MLA_NOPE), F32), w_in[:, o5:],
                             jnp.zeros((d, MLA_QK_PAD - MLA_NOPE - MLA_ROPE), F32)], axis=1)
    dq_w = w_in[:, :dw] * (DIFF_HEAD_DIM ** -0.5 * LOG2E)
    win = jnp.concatenate([dq_w, w_in[:, dw:o5], kpe_w], axis=1).astype(BF16)
    wq = lp["mla_w_q_up"] * ((MLA_NOPE + MLA_ROPE) ** -0.5 * LOG2E)
    wq = wq.reshape(q_lora, hm, MLA_NOPE + MLA_ROPE)
    wq = jnp.pad(wq, ((0, 0), (0, 0), (0, MLA_QK_PAD - MLA_NOPE - MLA_ROPE)))
    wq = wq.reshape(q_lora, hm * MLA_QK_PAD).astype(BF16)
    wkv = lp["mla_w_kv_up"].reshape(kv_lora, hm, MLA_NOPE + MLA_V)
    wk = jnp.pad(wkv[:, :, :MLA_NOPE], ((0, 0), (0, 0), (0, MLA_QK_PAD - MLA_NOPE)))
    wk = wk.reshape(kv_lora, hm * MLA_QK_PAD).astype(BF16)
    wv = wkv[:, :, MLA_NOPE:].reshape(kv_lora, hm * MLA_V).astype(BF16)
    rc, rsa, rsb = _rope_tables(s)

    qkv, qm, km, vm = _ln_proj(xp, xs, mod3, win, lp["mla_q_norm_g"].reshape(1, -1), wq,
                               lp["mla_kv_norm_g"].reshape(1, -1), wk, wv, rc, rsa, rsb, tm)

    lam_init = 0.8 - 0.6 * math.exp(-0.3 * layer_idx)
    od = _diff_attn(qkv, lp["diff_lambda_q1"].reshape(1, -1), lp["diff_lambda_k1"].reshape(1, -1),
                    lp["diff_lambda_q2"].reshape(1, -1), lp["diff_lambda_k2"].reshape(1, -1),
                    lp["diff_subln_g"].reshape(1, -1), nb, s, dw, lam_init, tq)
    om = _mla_attn(qm, km, vm, nb, s, tq)

    x1, h2, idx, gate, rank, cnt = _post(
        od, om, xp, xs, mod3, lp["w_out"].astype(BF16), lp["ln1_g"].reshape(1, -1),
        lp["ln1_b"].reshape(1, -1), lp["w_router"].astype(BF16), lp["b_router"].reshape(1, -1),
        alpha, tm)

    te = min(512, t * TOP_K // ne)
    nblk = (t * TOP_K) // te + ne
    p = nblk * te
    counts = cnt[0].astype(I32)
    padded = (counts + te - 1) // te * te
    pends = jnp.cumsum(padded)
    pstarts = pends - padded
    idx4 = idx[:, :TOP_K]
    onehot = idx4[:, :, None] == jnp.arange(ne, dtype=I32)[None, None, :]
    dest = jnp.sum(jnp.where(onehot, pstarts[None, None, :], 0), axis=-1) + rank[:, :TOP_K]
    dest_flat = dest.reshape(-1).astype(I32)
    buf_tok = jnp.zeros((p,), I32).at[dest_flat].set(jnp.arange(t * TOP_K, dtype=I32) // TOP_K)
    n_used = (pends[-1] // te).astype(I32)
    blk = jnp.minimum(jnp.arange(nblk, dtype=I32), n_used - 1)
    block_e = jnp.minimum(jnp.sum((pends[None, :] <= (blk * te)[:, None]).astype(I32), axis=1), ne - 1)

    b1 = lp["b_mlp1"]
    wgt, wut = _split_w1(lp["w_mlp1"], 256)
    bg = b1[:, 0::2].reshape(ne, 1, f)
    bu = b1[:, 1::2].reshape(ne, 1, f)
    eo = _experts(block_e, n_used.reshape(1), buf_tok, h2, wgt, wut, bg, bu,
                  lp["w_mlp2"].astype(BF16), lp["b_mlp2"].reshape(ne, 1, d), te, min(512, f))

    g2 = lp["ln2_g"].reshape(1, -1)
    b2 = lp["ln2_b"].reshape(1, -1)
    yp = _final(dest_flat, x1, mod3, gate, eo, g2, b2, alpha, 0, nbp, s, tm)
    ys = _final(dest_flat, x1, mod3, gate, eo, g2, b2, alpha, nbp, nb - nbp, s, tm)
    return yp, ys


def kernel(x_prompt, x_sample, c_prompt, c_sample, w_ada, b_ada, w_in, diff_lambda_q1, diff_lambda_k1, diff_lambda_q2, diff_lambda_k2, diff_subln_g, mla_q_norm_g, mla_w_q_up, mla_kv_norm_g, mla_w_kv_up, w_out, ln1_g, ln1_b, w_router, b_router, w_mlp1, b_mlp1, w_mlp2, b_mlp2, ln2_g, ln2_b):
    weights = dict(
        w_ada=w_ada, b_ada=b_ada, w_in=w_in, diff_lambda_q1=diff_lambda_q1,
        diff_lambda_k1=diff_lambda_k1, diff_lambda_q2=diff_lambda_q2, diff_lambda_k2=diff_lambda_k2,
        diff_subln_g=diff_subln_g, mla_q_norm_g=mla_q_norm_g, mla_w_q_up=mla_w_q_up,
        mla_kv_norm_g=mla_kv_norm_g, mla_w_kv_up=mla_w_kv_up, w_out=w_out, ln1_g=ln1_g, ln1_b=ln1_b,
        w_router=w_router, b_router=b_router, w_mlp1=w_mlp1, b_mlp1=b_mlp1, w_mlp2=w_mlp2,
        b_mlp2=b_mlp2, ln2_g=ln2_g, ln2_b=ln2_b)
    depth = w_ada.shape[0]
    assert x_prompt.shape[1:] == x_sample.shape[1:], "the two request groups are batched together"
    nbp, nbs = x_prompt.shape[0], x_sample.shape[0]
    nb = nbp + nbs
    d = x_prompt.shape[2]
    alpha = (2.0 * depth) ** 0.25
    bpad = -(-nb // 8) * 8
    c_pad = jnp.pad(jnp.concatenate([c_prompt, c_sample], axis=0), ((0, bpad - nb), (0, 0)))
    xp, xs = x_prompt, x_sample
    for i in range(depth):
        lp = {name: arr[i] for name, arr in weights.items()}
        mod3 = _ada(c_pad, lp["w_ada"], lp["b_ada"]).reshape(bpad, 1, 6 * d)
        xp, xs = _layer(xp, xs, mod3, lp, i, alpha)
    return (xp, xs)
```

```python
import functools
import math

import jax
import jax.numpy as jnp
from jax import lax
from jax.experimental import pallas as pl
from jax.experimental.pallas import tpu as pltpu

F32 = jnp.float32
BF16 = jnp.bfloat16
I32 = jnp.int32
U32 = jnp.uint32

DIFF_HEAD_DIM = 128
MLA_NOPE = 128
MLA_ROPE = 64
MLA_V = 128
ROPE_THETA = 10000.0
TOP_K = 4
SWIGLU_LIMIT = 7.0
SWIGLU_ALPHA = 1.702
LN_EPS = 1e-5
RMS_EPS = 1e-6
LOG2E = math.log2(math.e)

LANES = 128
MLA_QK_PAD = 2 * LANES
ROPE_HALF = MLA_ROPE // 2
VMEM_LIMIT = 56 * 1024 * 1024


def _cparams(sem, vmem=VMEM_LIMIT):
    return pltpu.CompilerParams(dimension_semantics=sem, vmem_limit_bytes=vmem)


def _ln(x):
    mu = jnp.mean(x, axis=-1, keepdims=True)
    xc = x - mu
    var = jnp.mean(xc * xc, axis=-1, keepdims=True)
    return xc * lax.rsqrt(var + LN_EPS)


def _rms(x, g):
    return x * lax.rsqrt(jnp.mean(x * x, axis=-1, keepdims=True) + RMS_EPS) * g


def _dot(a, b):
    return jnp.dot(a, b, preferred_element_type=F32)


def _dot_nt(a, b):
    return lax.dot_general(a, b, (((1,), (1,)), ((), ())), preferred_element_type=F32)


def _pack_bf16_pairs(x):
    half = x.shape[1] // 2
    xr = x.astype(jnp.bfloat16).astype(F32)
    hi = lax.bitcast_convert_type(xr[:, :half], U32)
    lo = lax.bitcast_convert_type(xr[:, half:], U32)
    return hi | (lo >> 16)


def _unpack_bf16_pairs(p):
    hi = lax.bitcast_convert_type(p & jnp.uint32(0xFFFF0000), F32)
    lo = lax.bitcast_convert_type(p << 16, F32)
    return hi, lo


def _softmax_pv(s2, v):
    m = jnp.max(s2, axis=-1, keepdims=True)
    e = jnp.exp2(s2 - m)
    inv = 1.0 / jnp.sum(e, axis=-1, keepdims=True)
    return _dot(e.astype(BF16), v) * inv


def _ada_kernel(c_ref, w_ref, b_ref, o_ref):
    c = c_ref[...]
    s = c * jax.nn.sigmoid(c)
    o_ref[...] = _dot(s.astype(BF16), w_ref[...].astype(BF16)) + b_ref[...]


def _ada(c_pad, w_ada, b_ada, tn=1024):
    bp, d = c_pad.shape
    n = w_ada.shape[1]
    tn = min(tn, n)
    return pl.pallas_call(
        _ada_kernel,
        grid=(n // tn,),
        in_specs=[pl.BlockSpec((bp, d), lambda j: (0, 0)),
                  pl.BlockSpec((d, tn), lambda j: (0, j)),
                  pl.BlockSpec((1, tn), lambda j: (0, j))],
        out_specs=pl.BlockSpec((bp, tn), lambda j: (0, j)),
        out_shape=jax.ShapeDtypeStruct((bp, n), F32),
        compiler_params=_cparams(("arbitrary",)),
        name="ada",
    )(c_pad, w_ada, b_ada.reshape(1, n))


def _rope_pad(x, c, sa, sb):
    return (x * c + pltpu.roll(x, ROPE_HALF, 1) * sa
            + pltpu.roll(x, MLA_QK_PAD - ROPE_HALF, 1) * sb)


def _ln_proj_kernel(nbp, d, dw, q_lora, kv_lora, hm,
                    xp_ref, xs_ref, mod_ref, win_ref, gq_ref, wq_ref, gkv_ref, wk_ref, wv_ref,
                    c_ref, sa_ref, sb_ref,
                    qkv_ref, qm_ref, km_ref, vm_ref):
    b = pl.program_id(0)
    x = jnp.where(b < nbp, xp_ref[0], xs_ref[0])
    mod = mod_ref[0]
    h = _ln(x) * (1.0 + mod[:, d:2 * d]) + mod[:, 0:d]
    hb = h.astype(BF16)
    o3 = 3 * dw
    qkv_ref[...] = _dot(hb, win_ref[:, 0:o3]).astype(BF16)
    cq = _dot(hb, win_ref[:, o3:o3 + q_lora])
    ckv = _dot(hb, win_ref[:, o3 + q_lora:o3 + q_lora + kv_lora])
    kpe = _dot(hb, win_ref[:, o3 + q_lora + kv_lora:o3 + q_lora + kv_lora + MLA_QK_PAD])
    c, sa, sb = c_ref[...], sa_ref[...], sb_ref[...]
    q = _dot(_rms(cq, gq_ref[...]).astype(BF16), wq_ref[...])
    ckvn = _rms(ckv, gkv_ref[...]).astype(BF16)
    kk = _dot(ckvn, wk_ref[...])
    vm_ref[...] = _dot(ckvn, wv_ref[...]).astype(BF16)
    kp = _rope_pad(kpe, c, sa, sb)
    for hh in range(hm):
        sl = slice(hh * MLA_QK_PAD, (hh + 1) * MLA_QK_PAD)
        qm_ref[:, sl] = _rope_pad(q[:, sl], c, sa, sb).astype(BF16)
        km_ref[:, sl] = (kk[:, sl] + kp).astype(BF16)


def _ln_proj(xp, xs, mod3, win, gq, wq, gkv, wk, wv, rc, rsa, rsb, tm):
    nbp, s, d = xp.shape
    nbs = xs.shape[0]
    nb = nbp + nbs
    dw = d // 2
    q_lora, kv_lora = gq.shape[1], gkv.shape[1]
    hm = wv.shape[1] // MLA_V
    nst = s // tm
    t = nb * s
    const = lambda bb, ss: (0, 0)
    row = lambda bb, ss: (bb * nst + ss, 0)
    once = dict(pipeline_mode=pl.Buffered(1))
    kern = functools.partial(_ln_proj_kernel, nbp, d, dw, q_lora, kv_lora, hm)
    return pl.pallas_call(
        kern,
        grid=(nb, nst),
        in_specs=[
            pl.BlockSpec((1, tm, d), lambda bb, ss: (jnp.minimum(bb, nbp - 1), ss, 0)),
            pl.BlockSpec((1, tm, d), lambda bb, ss: (jnp.maximum(bb - nbp, 0), ss, 0)),
            pl.BlockSpec((1, 1, 6 * d), lambda bb, ss: (bb, 0, 0)),
            pl.BlockSpec(win.shape, const, **once),
            pl.BlockSpec(gq.shape, const),
            pl.BlockSpec(wq.shape, const, **once),
            pl.BlockSpec(gkv.shape, const),
            pl.BlockSpec(wk.shape, const, **once),
            pl.BlockSpec(wv.shape, const, **once),
            pl.BlockSpec((tm, MLA_QK_PAD), lambda bb, ss: (ss, 0)),
            pl.BlockSpec((tm, MLA_QK_PAD), lambda bb, ss: (ss, 0)),
            pl.BlockSpec((tm, MLA_QK_PAD), lambda bb, ss: (ss, 0)),
        ],
        out_specs=[
            pl.BlockSpec((tm, 3 * dw), row),
            pl.BlockSpec((tm, hm * MLA_QK_PAD), row),
            pl.BlockSpec((tm, hm * MLA_QK_PAD), row),
            pl.BlockSpec((tm, hm * MLA_V), row),
        ],
        out_shape=[
            jax.ShapeDtypeStruct((t, 3 * dw), BF16),
            jax.ShapeDtypeStruct((t, hm * MLA_QK_PAD), BF16),
            jax.ShapeDtypeStruct((t, hm * MLA_QK_PAD), BF16),
            jax.ShapeDtypeStruct((t, hm * MLA_V), BF16),
        ],
        compiler_params=_cparams(("arbitrary", "arbitrary")),
        name="ln_proj",
    )(xp, xs, mod3, win, gq, wq, gkv, wk, wv, rc, rsa, rsb)


def _diff_kernel(hd, tq, s, lam_init,
                 q_ref, k_ref, v_ref, lq1_ref, lk1_ref, lq2_ref, lk2_ref, g_ref, o_ref):
    lam = (jnp.exp(jnp.sum(lq1_ref[...] * lk1_ref[...], axis=-1, keepdims=True))
           - jnp.exp(jnp.sum(lq2_ref[...] * lk2_ref[...], axis=-1, keepdims=True)) + lam_init)
    q0 = pl.program_id(1) * tq
    rowp = lax.broadcasted_iota(I32, (tq, s), 0) + q0
    colp = lax.broadcasted_iota(I32, (tq, s), 1)
    dist = jnp.abs(rowp - colp).astype(F32)
    dv = 2 * DIFF_HEAD_DIM
    g = g_ref[...]
    for h in range(hd):
        bias = dist * (LOG2E * 2.0 ** (-8.0 * (h + 1) / hd))
        v = v_ref[:, h * dv:(h + 1) * dv]
        om = []
        for m in range(2):
            sl = slice(h * dv + m * DIFF_HEAD_DIM, h * dv + (m + 1) * DIFF_HEAD_DIM)
            om.append(_softmax_pv(_dot_nt(q_ref[:, sl], k_ref[:, sl]) - bias, v))
        o = om[0] - lam * om[1]
        o_ref[:, h * dv:(h + 1) * dv] = (_rms(o, g) * (1.0 - lam_init)).astype(BF16)


def _diff_attn(qkv, lq1, lk1, lq2, lk2, g, nb, s, dw, lam_init, tq):
    hd = dw // (2 * DIFF_HEAD_DIM)
    nq = s // tq
    vec = pl.BlockSpec((1, DIFF_HEAD_DIM), lambda b, i: (0, 0))
    kern = functools.partial(_diff_kernel, hd, tq, s, lam_init)
    return pl.pallas_call(
        kern,
        grid=(nb, nq),
        in_specs=[pl.BlockSpec((tq, dw), lambda b, i: (b * nq + i, 0)),
                  pl.BlockSpec((s, dw), lambda b, i: (b, 1)),
                  pl.BlockSpec((s, dw), lambda b, i: (b, 2)),
                  vec, vec, vec, vec,
                  pl.BlockSpec((1, 2 * DIFF_HEAD_DIM), lambda b, i: (0, 0))],
        out_specs=pl.BlockSpec((tq, dw), lambda b, i: (b * nq + i, 0)),
        out_shape=jax.ShapeDtypeStruct((nb * s, dw), BF16),
        compiler_params=_cparams(("arbitrary", "arbitrary")),
        name="diff_attn",
    )(qkv, qkv, qkv, lq1, lk1, lq2, lk2, g)


def _mla_kernel(hm, q_ref, k_ref, v_ref, o_ref):
    for h in range(hm):
        sl = slice(h * MLA_QK_PAD, (h + 1) * MLA_QK_PAD)
        o_ref[:, h * MLA_V:(h + 1) * MLA_V] = _softmax_pv(
            _dot_nt(q_ref[:, sl], k_ref[:, sl]), v_ref[:, h * MLA_V:(h + 1) * MLA_V]).astype(BF16)


def _mla_attn(qm, km, vm, nb, s, tq):
    hm = vm.shape[1] // MLA_V
    nq = s // tq
    return pl.pallas_call(
        functools.partial(_mla_kernel, hm),
        grid=(nb, nq),
        in_specs=[pl.BlockSpec((tq, hm * MLA_QK_PAD), lambda b, i: (b * nq + i, 0)),
                  pl.BlockSpec((s, hm * MLA_QK_PAD), lambda b, i: (b, 0)),
                  pl.BlockSpec((s, hm * MLA_V), lambda b, i: (b, 0))],
        out_specs=pl.BlockSpec((tq, hm * MLA_V), lambda b, i: (b * nq + i, 0)),
        out_shape=jax.ShapeDtypeStruct((nb * s, hm * MLA_V), BF16),
        compiler_params=_cparams(("arbitrary", "arbitrary")),
        name="mla_attn",
    )(qm, km, vm)


def _post_kernel(nbp, d, dw, ne, tm, alpha,
                 od_ref, om_ref, xp_ref, xs_ref, mod_ref, wo_ref, g1_ref, b1_ref, wr_ref, br_ref,
                 x1_ref, h2_ref, idx_ref, gate_ref, rank_ref, cnt_ref, run_ref):
    b = pl.program_id(0)
    first = jnp.logical_and(b == 0, pl.program_id(1) == 0)

    @pl.when(first)
    def _():
        run_ref[...] = jnp.zeros_like(run_ref)

    mix = _dot(od_ref[...], wo_ref[0:dw, :]) + _dot(om_ref[...], wo_ref[dw:2 * dw, :])
    x = jnp.where(b < nbp, xp_ref[0], xs_ref[0])
    mod = mod_ref[0]
    x1 = _ln(alpha * x + mod[:, 2 * d:3 * d] * mix) * g1_ref[...] + b1_ref[...]
    x1_ref[...] = x1
    h2 = _ln(x1) * (1.0 + mod[:, 4 * d:5 * d]) + mod[:, 3 * d:4 * d]
    h2_ref[...] = _pack_bf16_pairs(h2)
    logits = _dot(h2.astype(BF16), wr_ref[...]) + br_ref[...]

    lane = lax.broadcasted_iota(I32, (tm, ne), 1).astype(F32)
    work = logits
    vals, idxs = [], []
    for _ in range(TOP_K):
        m = jnp.max(work, axis=-1, keepdims=True)
        i = jnp.min(jnp.where(work == m, lane, float(ne)), axis=-1, keepdims=True)
        vals.append(m)
        idxs.append(i)
        work = jnp.where(lane == i, -jnp.inf, work)
    es = [jnp.exp(v - vals[0]) for v in vals]
    inv = 1.0 / (es[0] + es[1] + es[2] + es[3])

    ohs = [jnp.where(lane == i, 1.0, 0.0) for i in idxs]
    oh = ohs[0] + ohs[1] + ohs[2] + ohs[3]
    tri = jnp.where(lax.broadcasted_iota(I32, (tm, tm), 0) > lax.broadcasted_iota(I32, (tm, tm), 1),
                    1.0, 0.0)
    before = _dot(tri.astype(BF16), oh.astype(BF16)) + run_ref[...]
    run_ref[...] = run_ref[...] + jnp.sum(oh, axis=0, keepdims=True)
    cnt_ref[...] = run_ref[...]

    lane_o = lax.broadcasted_iota(I32, (tm, LANES), 1)
    idx_o = jnp.zeros((tm, LANES), I32)
    gate_o = jnp.zeros((tm, LANES), F32)
    rank_o = jnp.zeros((tm, LANES), I32)
    for k in range(TOP_K):
        rk = jnp.sum(ohs[k] * before, axis=-1, keepdims=True).astype(I32)
        idx_o = jnp.where(lane_o == k, idxs[k].astype(I32), idx_o)
        gate_o = jnp.where(lane_o == k, es[k] * inv, gate_o)
        rank_o = jnp.where(lane_o == k, rk, rank_o)
    idx_ref[...] = idx_o
    gate_ref[...] = gate_o
    rank_ref[...] = rank_o


def _post(od, om, xp, xs, mod3, wo, g1, b1, wr, br, alpha, tm):
    nbp, s, d = xp.shape
    nb = nbp + xs.shape[0]
    dw = d // 2
    ne = wr.shape[1]
    nst = s // tm
    t = nb * s
    const = lambda bb, ss: (0, 0)
    row = lambda bb, ss: (bb * nst + ss, 0)
    kern = functools.partial(_post_kernel, nbp, d, dw, ne, tm, alpha)
    return pl.pallas_call(
        kern,
        grid=(nb, nst),
        in_specs=[
            pl.BlockSpec((tm, dw), row),
            pl.BlockSpec((tm, dw), row),
            pl.BlockSpec((1, tm, d), lambda bb, ss: (jnp.minimum(bb, nbp - 1), ss, 0)),
            pl.BlockSpec((1, tm, d), lambda bb, ss: (jnp.maximum(bb - nbp, 0), ss, 0)),
            pl.BlockSpec((1, 1, 6 * d), lambda bb, ss: (bb, 0, 0)),
            pl.BlockSpec(wo.shape, const, pipeline_mode=pl.Buffered(1)),
            pl.BlockSpec((1, d), const),
            pl.BlockSpec((1, d), const),
            pl.BlockSpec(wr.shape, const),
            pl.BlockSpec((1, ne), const),
        ],
        out_specs=[
            pl.BlockSpec((tm, d), row),
            pl.BlockSpec((tm, d // 2), row),
            pl.BlockSpec((tm, LANES), row),
            pl.BlockSpec((tm, LANES), row),
            pl.BlockSpec((tm, LANES), row),
            pl.BlockSpec((1, ne), const),
        ],
        out_shape=[
            jax.ShapeDtypeStruct((t, d), F32),
            jax.ShapeDtypeStruct((t, d // 2), U32),
            jax.ShapeDtypeStruct((t, LANES), I32),
            jax.ShapeDtypeStruct((t, LANES), F32),
            jax.ShapeDtypeStruct((t, LANES), I32),
            jax.ShapeDtypeStruct((1, ne), F32),
        ],
        scratch_shapes=[pltpu.VMEM((1, ne), F32)],
        compiler_params=_cparams(("arbitrary", "arbitrary")),
        name="post_mixer",
    )(od, om, xp, xs, mod3, wo, g1, b1, wr, br)


ROW_BLOCKS_AHEAD = 2
ROW_SLOTS = ROW_BLOCKS_AHEAD + 1
ROW_DMA_PRIORITY = 1


def _expert_kernel(tm, nblk, nf,
                   be_ref, nu_ref, tok0_ref, tok1_ref, tokn_ref, src_ref, wg_ref, wu_ref, bg_ref, bu_ref,
                   w2_ref, b2_ref, o_ref, xg_ref, xb_ref, sem):
    i = pl.program_id(0)
    j = pl.program_id(1)
    n_used = nu_ref[0]
    used = i < n_used
    slot = i % ROW_SLOTS
    ahead_slot = (i + ROW_BLOCKS_AHEAD) % ROW_SLOTS
    rows_per_step = tm // nf

    def row_start(tok_ref, r, s):
        pltpu.make_async_copy(src_ref.at[pl.ds(tok_ref[r], 1)], xg_ref.at[s, pl.ds(r, 1)],
                              sem.at[s]).start(priority=ROW_DMA_PRIORITY)

    def slot_wait(s):
        pltpu.make_async_copy(src_ref.at[pl.ds(0, tm)], xg_ref.at[s], sem.at[s]).wait()

    @pl.when(jnp.logical_and(i == 0, j == 0))
    def _():
        def issue(r, carry):
            row_start(tok0_ref, r, 0)
            row_start(tok1_ref, r, 1)
            return carry
        lax.fori_loop(0, tm, issue, 0, unroll=8)

    @pl.when(jnp.logical_and(j == 0, i <= n_used + (ROW_BLOCKS_AHEAD - 1)))
    def _():
        slot_wait(slot)
        hi, lo = _unpack_bf16_pairs(xg_ref[slot])
        half = hi.shape[1]
        xb_ref[:, :half] = hi.astype(BF16)
        xb_ref[:, half:] = lo.astype(BF16)

    @pl.when(j == 0)
    def _():
        o_ref[...] = jnp.broadcast_to(b2_ref[0], o_ref.shape)

    @pl.when(used)
    def _():
        for r in range(rows_per_step):
            row_start(tokn_ref, j * rows_per_step + r, ahead_slot)
        xb = xb_ref[...]
        hg = _dot_nt(xb, wg_ref[0]) + bg_ref[0]
        hu = _dot_nt(xb, wu_ref[0]) + bu_ref[0]
        glu = jnp.minimum(hg, SWIGLU_LIMIT)
        lin = jnp.clip(hu, -SWIGLU_LIMIT, SWIGLU_LIMIT)
        act = glu * jax.nn.sigmoid(SWIGLU_ALPHA * glu) * (lin + 1.0)
        o_ref[...] += _dot(act.astype(BF16), w2_ref[0])

    last = jnp.logical_and(i == nblk - 1, j == nf - 1)
    filled_nblk = (n_used > nblk - 2) if nblk >= 2 else True

    @pl.when(jnp.logical_and(last, filled_nblk))
    def _():
        slot_wait(nblk % ROW_SLOTS)

    @pl.when(jnp.logical_and(last, n_used > nblk - 1))
    def _():
        slot_wait((nblk + 1) % ROW_SLOTS)


def _experts(block_e, n_used, tok, src, wgt, wut, bg, bu, w2, b2, tm, tf):
    ne, f, d = wgt.shape
    p = tok.shape[0]
    nblk = p // tm
    nf = f // tf

    def wtile(i, j, be, nu):
        return (be[i], jnp.where(i < nu[0], j, nf - 1), 0)

    def btile(i, j, be, nu):
        return (be[i], 0, jnp.where(i < nu[0], j, nf - 1))

    grid_spec = pltpu.PrefetchScalarGridSpec(
        num_scalar_prefetch=2,
        grid=(nblk, nf),
        in_specs=[
            pl.BlockSpec((tm,), lambda i, j, be, nu: (0,), memory_space=pltpu.SMEM),
            pl.BlockSpec((tm,), lambda i, j, be, nu: (min(1, nblk - 1),), memory_space=pltpu.SMEM),
            pl.BlockSpec((tm,), lambda i, j, be, nu: (jnp.minimum(i + ROW_BLOCKS_AHEAD, nblk - 1),),
                         memory_space=pltpu.SMEM),
            pl.BlockSpec(memory_space=pl.ANY),
            pl.BlockSpec((1, tf, d), wtile),
            pl.BlockSpec((1, tf, d), wtile),
            pl.BlockSpec((1, 1, tf), btile),
            pl.BlockSpec((1, 1, tf), btile),
            pl.BlockSpec((1, tf, d), wtile),
            pl.BlockSpec((1, 1, d), lambda i, j, be, nu: (be[i], 0, 0)),
        ],
        out_specs=pl.BlockSpec((tm, d), lambda i, j, be, nu: (i, 0)),
        scratch_shapes=[pltpu.VMEM((ROW_SLOTS, tm, d // 2), U32), pltpu.VMEM((tm, d), BF16),
                        pltpu.SemaphoreType.DMA((ROW_SLOTS,))],
    )
    return pl.pallas_call(
        functools.partial(_expert_kernel, tm, nblk, nf),
        grid_spec=grid_spec,
        out_shape=jax.ShapeDtypeStruct((p, d), F32),
        compiler_params=_cparams(("arbitrary", "arbitrary")),
        name="experts",
    )(block_e, n_used, tok, tok, tok, src, wgt, wut, bg, bu, w2, b2)


def _split_w1_kernel(f, x_ref, g_ref, u_ref, t_ref):
    xt = x_ref[0].T
    for c in range(t_ref.shape[0]):
        sl = slice(c * LANES, (c + 1) * LANES)
        t_ref[c] = xt[:, sl]
        g_ref[0, :, sl] = t_ref[c, pl.ds(0, f, stride=2), :].astype(BF16)
        u_ref[0, :, sl] = t_ref[c, pl.ds(1, f, stride=2), :].astype(BF16)


def _split_w1(w1, td):
    ne, d, f2 = w1.shape
    f = f2 // 2
    td = min(td, d)
    out = jax.ShapeDtypeStruct((ne, f, d), BF16)
    return pl.pallas_call(
        functools.partial(_split_w1_kernel, f),
        grid=(ne, d // td),
        in_specs=[pl.BlockSpec((1, td, f2), lambda e, i: (e, i, 0))],
        out_specs=[pl.BlockSpec((1, f, td), lambda e, i: (e, 0, i)),
                   pl.BlockSpec((1, f, td), lambda e, i: (e, 0, i))],
        out_shape=[out, out],
        scratch_shapes=[pltpu.VMEM((td // LANES, f2, LANES), F32)],
        compiler_params=_cparams(("arbitrary", "arbitrary")),
        name="split_w1",
    )(w1)


def _final_kernel(d, tm, alpha,
                  dest_ref, x1_ref, mod_ref, gate_ref, src_ref, g2_ref, b2_ref, o_ref,
                  rows_ref, sem):
    def issue(r, carry):
        for k in range(TOP_K):
            pltpu.make_async_copy(src_ref.at[pl.ds(dest_ref[r * TOP_K + k], 1)],
                                  rows_ref.at[k, pl.ds(r, 1)], sem).start(priority=k % 2)
        return carry

    lax.fori_loop(0, tm, issue, 0, unroll=4)
    for k in range(TOP_K):
        pltpu.make_async_copy(src_ref.at[pl.ds(0, tm)], rows_ref.at[k], sem).wait()
    gate = gate_ref[...]
    y = gate[:, 0:1] * rows_ref[0]
    for k in range(1, TOP_K):
        y = y + gate[:, k:k + 1] * rows_ref[k]
    mod = mod_ref[0]
    z = alpha * x1_ref[...] + mod[:, 5 * d:6 * d] * y
    o_ref[0] = _ln(z) * g2_ref[...] + b2_ref[...]


def _final(dest_flat, x1, mod3, gate, eo, g2, b2, alpha, b0, nbg, s, tm):
    d = x1.shape[1]
    nst = s // tm
    off = b0 * nst
    kern = functools.partial(_final_kernel, d, tm, alpha)
    return pl.pallas_call(
        kern,
        grid=(nbg * nst,),
        in_specs=[
            pl.BlockSpec((tm * TOP_K,), lambda i: (i + off,), memory_space=pltpu.SMEM),
            pl.BlockSpec((tm, d), lambda i: (i + off, 0)),
            pl.BlockSpec((1, 1, 6 * d), lambda i: (b0 + i // nst, 0, 0)),
            pl.BlockSpec((tm, LANES), lambda i: (i + off, 0)),
            pl.BlockSpec(memory_space=pl.ANY),
            pl.BlockSpec((1, d), lambda i: (0, 0)),
            pl.BlockSpec((1, d), lambda i: (0, 0)),
        ],
        out_specs=pl.BlockSpec((1, tm, d), lambda i: (i // nst, i % nst, 0)),
        out_shape=jax.ShapeDtypeStruct((nbg, s, d), F32),
        scratch_shapes=[pltpu.VMEM((TOP_K, tm, d), F32), pltpu.SemaphoreType.DMA(())],
        compiler_params=_cparams(("arbitrary",)),
        name="final_combine",
    )(dest_flat, x1, mod3, gate, eo, g2, b2)


def _rope_tables(s):
    inv = 1.0 / (ROPE_THETA ** (jnp.arange(0, MLA_ROPE, 2, dtype=F32) / MLA_ROPE))
    ang = jnp.arange(s, dtype=F32)[:, None] * inv[None, :]
    cos, sin = jnp.cos(ang), jnp.sin(ang)
    one = jnp.ones((s, MLA_NOPE), F32)
    z32 = jnp.zeros((s, ROPE_HALF), F32)
    ztail = jnp.zeros((s, MLA_QK_PAD - MLA_NOPE - MLA_ROPE), F32)
    znope = jnp.zeros((s, MLA_NOPE), F32)
    c = jnp.concatenate([one, cos, cos, ztail], axis=1)
    sa = jnp.concatenate([znope, z32, sin, ztail], axis=1)
    sb = jnp.concatenate([znope, -sin, z32, ztail], axis=1)
    return c, sa, sb


def _pad_cols(w, width):
    return jnp.pad(w, ((0, 0), (0, width - w.shape[1])))


def _layer(xp, xs, mod3, lp, layer_idx, alpha):
    nbp, s, d = xp.shape
    nb = nbp + xs.shape[0]
    dw = d // 2
    hm = (d - dw) // MLA_V
    ne = lp["w_router"].shape[1]
    f = lp["w_mlp2"].shape[1]
    t = nb * s
    tm = min(256, s)
    tq = min(512, s)

    w_in = lp["w_in"]
    q_lora = lp["mla_q_norm_g"].shape[0]
    kv_lora = lp["mla_kv_norm_g"].shape[0]
    o3, o4, o5 = 3 * dw, 3 * dw + q_lora, 3 * dw + q_lora + kv_lora
    kpe_w = jnp.concatenate([jnp.zeros((d, MLA_NOPE), F32), w_in[:, o5:],
                             jnp.zeros((d, MLA_QK_PAD - MLA_NOPE - MLA_ROPE), F32)], axis=1)
    dq_w = w_in[:, :dw] * (DIFF_HEAD_DIM ** -0.5 * LOG2E)
    win = jnp.concatenate([dq_w, w_in[:, dw:o5], kpe_w], axis=1).astype(BF16)
    wq = lp["mla_w_q_up"] * ((MLA_NOPE + MLA_ROPE) ** -0.5 * LOG2E)
    wq = wq.reshape(q_lora, hm, MLA_NOPE + MLA_ROPE)
    wq = jnp.pad(wq, ((0, 0), (0, 0), (0, MLA_QK_PAD - MLA_NOPE - MLA_ROPE)))
    wq = wq.reshape(q_lora, hm * MLA_QK_PAD).astype(BF16)
    wkv = lp["mla_w_kv_up"].reshape(kv_lora, hm, MLA_NOPE + MLA_V)
    wk = jnp.pad(wkv[:, :, :MLA_NOPE], ((0, 0), (0, 0), (0, MLA_QK_PAD - MLA_NOPE)))
    wk = wk.reshape(kv_lora, hm * MLA_QK_PAD).astype(BF16)
    wv = wkv[:, :, MLA_NOPE:].reshape(kv_lora, hm * MLA_V).astype(BF16)
    rc, rsa, rsb = _rope_tables(s)

    qkv, qm, km, vm = _ln_proj(xp, xs, mod3, win, lp["mla_q_norm_g"].reshape(1, -1), wq,
                               lp["mla_kv_norm_g"].reshape(1, -1), wk, wv, rc, rsa, rsb, tm)

    lam_init = 0.8 - 0.6 * math.exp(-0.3 * layer_idx)
    od = _diff_attn(qkv, lp["diff_lambda_q1"].reshape(1, -1), lp["diff_lambda_k1"].reshape(1, -1),
                    lp["diff_lambda_q2"].reshape(1, -1), lp["diff_lambda_k2"].reshape(1, -1),
                    lp["diff_subln_g"].reshape(1, -1), nb, s, dw, lam_init, tq)
    om = _mla_attn(qm, km, vm, nb, s, tq)

    x1, h2, idx, gate, rank, cnt = _post(
        od, om, xp, xs, mod3, lp["w_out"].astype(BF16), lp["ln1_g"].reshape(1, -1),
        lp["ln1_b"].reshape(1, -1), lp["w_router"].astype(BF16), lp["b_router"].reshape(1, -1),
        alpha, tm)

    te = min(512, t * TOP_K // ne)
    nblk = (t * TOP_K) // te + ne
    p = nblk * te
    counts = cnt[0].astype(I32)
    padded = (counts + te - 1) // te * te
    pends = jnp.cumsum(padded)
    pstarts = pends - padded
    idx4 = idx[:, :TOP_K]
    onehot = idx4[:, :, None] == jnp.arange(ne, dtype=I32)[None, None, :]
    dest = jnp.sum(jnp.where(onehot, pstarts[None, None, :], 0), axis=-1) + rank[:, :TOP_K]
    dest_flat = dest.reshape(-1).astype(I32)
    buf_tok = jnp.zeros((p,), I32).at[dest_flat].set(jnp.arange(t * TOP_K, dtype=I32) // TOP_K)
    n_used = (pends[-1] // te).astype(I32)
    blk = jnp.minimum(jnp.arange(nblk, dtype=I32), n_used - 1)
    block_e = jnp.minimum(jnp.sum((pends[None, :] <= (blk * te)[:, None]).astype(I32), axis=1), ne - 1)

    b1 = lp["b_mlp1"]
    wgt, wut = _split_w1(lp["w_mlp1"], 256)
    bg = b1[:, 0::2].reshape(ne, 1, f)
    bu = b1[:, 1::2].reshape(ne, 1, f)
    eo = _experts(block_e, n_used.reshape(1), buf_tok, h2, wgt, wut, bg, bu,
                  lp["w_mlp2"].astype(BF16), lp["b_mlp2"].reshape(ne, 1, d), te, min(512, f))

    g2 = lp["ln2_g"].reshape(1, -1)
    b2 = lp["ln2_b"].reshape(1, -1)
    yp = _final(dest_flat, x1, mod3, gate, eo, g2, b2, alpha, 0, nbp, s, tm)
    ys = _final(dest_flat, x1, mod3, gate, eo, g2, b2, alpha, nbp, nb - nbp, s, tm)
    return yp, ys


def kernel(x_prompt, x_sample, c_prompt, c_sample, w_ada, b_ada, w_in, diff_lambda_q1, diff_lambda_k1, diff_lambda_q2, diff_lambda_k2, diff_subln_g, mla_q_norm_g, mla_w_q_up, mla_kv_norm_g, mla_w_kv_up, w_out, ln1_g, ln1_b, w_router, b_router, w_mlp1, b_mlp1, w_mlp2, b_mlp2, ln2_g, ln2_b):
    weights = dict(
        w_ada=w_ada, b_ada=b_ada, w_in=w_in, diff_lambda_q1=diff_lambda_q1,
        diff_lambda_k1=diff_lambda_k1, diff_lambda_q2=diff_lambda_q2, diff_lambda_k2=diff_lambda_k2,
        diff_subln_g=diff_subln_g, mla_q_norm_g=mla_q_norm_g, mla_w_q_up=mla_w_q_up,
        mla_kv_norm_g=mla_kv_norm_g, mla_w_kv_up=mla_w_kv_up, w_out=w_out, ln1_g=ln1_g, ln1_b=ln1_b,
        w_router=w_router, b_router=b_router, w_mlp1=w_mlp1, b_mlp1=b_mlp1, w_mlp2=w_mlp2,
        b_mlp2=b_mlp2, ln2_g=ln2_g, ln2_b=ln2_b)
    depth = w_ada.shape[0]
    assert x_prompt.shape[1:] == x_sample.shape[1:], "the two request groups are batched together"
    nbp, nbs = x_prompt.shape[0], x_sample.shape[0]
    nb = nbp + nbs
    d = x_prompt.shape[2]
    alpha = (2.0 * depth) ** 0.25
    bpad = -(-nb // 8) * 8
    c_pad = jnp.pad(jnp.concatenate([c_prompt, c_sample], axis=0), ((0, bpad - nb), (0, 0)))
    xp, xs = x_prompt, x_sample
    for i in range(depth):
        lp = {name: arr[i] for name, arr in weights.items()}
        mod3 = _ada(c_pad, lp["w_ada"], lp["b_ada"]).reshape(bpad, 1, 6 * d)
        xp, xs = _layer(xp, xs, mod3, lp, i, alpha)
    return (xp, xs)
```

```python
import functools
import math

import jax
import jax.numpy as jnp
from jax import lax
from jax.experimental import pallas as pl
from jax.experimental.pallas import tpu as pltpu

F32 = jnp.float32
BF16 = jnp.bfloat16
I32 = jnp.int32
U32 = jnp.uint32

DIFF_HEAD_DIM = 128
MLA_NOPE = 128
MLA_ROPE = 64
MLA_V = 128
ROPE_THETA = 10000.0
TOP_K = 4
SWIGLU_LIMIT = 7.0
SWIGLU_ALPHA = 1.702
LN_EPS = 1e-5
RMS_EPS = 1e-6
LOG2E = math.log2(math.e)

LANES = 128
SUBLANES = 8
MLA_QK_PAD = 2 * LANES
ROPE_HALF = MLA_ROPE // 2
VMEM_LIMIT = 56 * 1024 * 1024


def _cparams(sem, vmem=VMEM_LIMIT):
    return pltpu.CompilerParams(dimension_semantics=sem, vmem_limit_bytes=vmem)


def _ln(x):
    mu = jnp.mean(x, axis=-1, keepdims=True)
    xc = x - mu
    var = jnp.mean(xc * xc, axis=-1, keepdims=True)
    return xc * lax.rsqrt(var + LN_EPS)


def _rms(x, g):
    return x * lax.rsqrt(jnp.mean(x * x, axis=-1, keepdims=True) + RMS_EPS) * g


def _dot(a, b):
    return jnp.dot(a, b, preferred_element_type=F32)


def _dot_nt(a, b):
    return lax.dot_general(a, b, (((1,), (1,)), ((), ())), preferred_element_type=F32)


def _pack_bf16_pairs(x):
    half = x.shape[1] // 2
    xr = x.astype(jnp.bfloat16).astype(F32)
    hi = lax.bitcast_convert_type(xr[:, :half], U32)
    lo = lax.bitcast_convert_type(xr[:, half:], U32)
    return hi | (lo >> 16)


def _unpack_bf16_pairs(p):
    hi = lax.bitcast_convert_type(p & jnp.uint32(0xFFFF0000), F32)
    lo = lax.bitcast_convert_type(p << 16, F32)
    return hi, lo


def _softmax_pv(s2, v):
    m = jnp.max(s2, axis=-1, keepdims=True)
    e = jnp.exp2(s2 - m)
    inv = 1.0 / jnp.sum(e, axis=-1, keepdims=True)
    return _dot(e.astype(BF16), v) * inv


def _ada_kernel(c_ref, w_ref, b_ref, o_ref):
    c = c_ref[...]
    s = c * jax.nn.sigmoid(c)
    o_ref[...] = _dot(s.astype(BF16), w_ref[...].astype(BF16)) + b_ref[...]


def _ada(c_pad, w_ada, b_ada, tn=1024):
    bp, d = c_pad.shape
    n = w_ada.shape[1]
    tn = min(tn, n)
    return pl.pallas_call(
        _ada_kernel,
        grid=(n // tn,),
        in_specs=[pl.BlockSpec((bp, d), lambda j: (0, 0)),
                  pl.BlockSpec((d, tn), lambda j: (0, j)),
                  pl.BlockSpec((1, tn), lambda j: (0, j))],
        out_specs=pl.BlockSpec((bp, tn), lambda j: (0, j)),
        out_shape=jax.ShapeDtypeStruct((bp, n), F32),
        compiler_params=_cparams(("arbitrary",)),
        name="ada",
    )(c_pad, w_ada, b_ada.reshape(1, n))


def _rope_pad(x, c, sa, sb):
    return (x * c + pltpu.roll(x, ROPE_HALF, 1) * sa
            + pltpu.roll(x, MLA_QK_PAD - ROPE_HALF, 1) * sb)


def _ln_proj_kernel(nbp, d, dw, q_lora, kv_lora, hm,
                    xp_ref, xs_ref, mod_ref, win_ref, gq_ref, wq_ref, gkv_ref, wk_ref, wv_ref,
                    c_ref, sa_ref, sb_ref,
                    qkv_ref, qm_ref, km_ref, vm_ref):
    b = pl.program_id(0)
    x = jnp.where(b < nbp, xp_ref[0], xs_ref[0])
    mod = mod_ref[0]
    h = _ln(x) * (1.0 + mod[:, d:2 * d]) + mod[:, 0:d]
    hb = h.astype(BF16)
    o3 = 3 * dw
    qkv_ref[...] = _dot(hb, win_ref[:, 0:o3]).astype(BF16)
    cq = _dot(hb, win_ref[:, o3:o3 + q_lora])
    ckv = _dot(hb, win_ref[:, o3 + q_lora:o3 + q_lora + kv_lora])
    kpe = _dot(hb, win_ref[:, o3 + q_lora + kv_lora:o3 + q_lora + kv_lora + MLA_QK_PAD])
    c, sa, sb = c_ref[...], sa_ref[...], sb_ref[...]
    q = _dot(_rms(cq, gq_ref[...]).astype(BF16), wq_ref[...])
    ckvn = _rms(ckv, gkv_ref[...]).astype(BF16)
    kk = _dot(ckvn, wk_ref[...])
    vm_ref[...] = _dot(ckvn, wv_ref[...]).astype(BF16)
    kp = _rope_pad(kpe, c, sa, sb)
    for hh in range(hm):
        sl = slice(hh * MLA_QK_PAD, (hh + 1) * MLA_QK_PAD)
        qm_ref[:, sl] = _rope_pad(q[:, sl], c, sa, sb).astype(BF16)
        km_ref[:, sl] = (kk[:, sl] + kp).astype(BF16)


def _ln_proj(xp, xs, mod3, win, gq, wq, gkv, wk, wv, rc, rsa, rsb, tm):
    nbp, s, d = xp.shape
    nbs = xs.shape[0]
    nb = nbp + nbs
    dw = d // 2
    q_lora, kv_lora = gq.shape[1], gkv.shape[1]
    hm = wv.shape[1] // MLA_V
    nst = s // tm
    t = nb * s
    const = lambda bb, ss: (0, 0)
    row = lambda bb, ss: (bb * nst + ss, 0)
    once = dict(pipeline_mode=pl.Buffered(1))
    kern = functools.partial(_ln_proj_kernel, nbp, d, dw, q_lora, kv_lora, hm)
    return pl.pallas_call(
        kern,
        grid=(nb, nst),
        in_specs=[
            pl.BlockSpec((1, tm, d), lambda bb, ss: (jnp.minimum(bb, nbp - 1), ss, 0)),
            pl.BlockSpec((1, tm, d), lambda bb, ss: (jnp.maximum(bb - nbp, 0), ss, 0)),
            pl.BlockSpec((1, 1, 6 * d), lambda bb, ss: (bb, 0, 0)),
            pl.BlockSpec(win.shape, const, **once),
            pl.BlockSpec(gq.shape, const),
            pl.BlockSpec(wq.shape, const, **once),
            pl.BlockSpec(gkv.shape, const),
            pl.BlockSpec(wk.shape, const, **once),
            pl.BlockSpec(wv.shape, const, **once),
            pl.BlockSpec((tm, MLA_QK_PAD), lambda bb, ss: (ss, 0)),
            pl.BlockSpec((tm, MLA_QK_PAD), lambda bb, ss: (ss, 0)),
            pl.BlockSpec((tm, MLA_QK_PAD), lambda bb, ss: (ss, 0)),
        ],
        out_specs=[
            pl.BlockSpec((tm, 3 * dw), row),
            pl.BlockSpec((tm, hm * MLA_QK_PAD), row),
            pl.BlockSpec((tm, hm * MLA_QK_PAD), row),
            pl.BlockSpec((tm, hm * MLA_V), row),
        ],
        out_shape=[
            jax.ShapeDtypeStruct((t, 3 * dw), BF16),
            jax.ShapeDtypeStruct((t, hm * MLA_QK_PAD), BF16),
            jax.ShapeDtypeStruct((t, hm * MLA_QK_PAD), BF16),
            jax.ShapeDtypeStruct((t, hm * MLA_V), BF16),
        ],
        compiler_params=_cparams(("arbitrary", "arbitrary")),
        name="ln_proj",
    )(xp, xs, mod3, win, gq, wq, gkv, wk, wv, rc, rsa, rsb)


def _diff_kernel(hd, tq, s, lam_init,
                 q_ref, k_ref, v_ref, lq1_ref, lk1_ref, lq2_ref, lk2_ref, g_ref, o_ref):
    lam = (jnp.exp(jnp.sum(lq1_ref[...] * lk1_ref[...], axis=-1, keepdims=True))
           - jnp.exp(jnp.sum(lq2_ref[...] * lk2_ref[...], axis=-1, keepdims=True)) + lam_init)
    q0 = pl.program_id(1) * tq
    rowp = lax.broadcasted_iota(I32, (tq, s), 0) + q0
    colp = lax.broadcasted_iota(I32, (tq, s), 1)
    dist = jnp.abs(rowp - colp).astype(F32)
    dv = 2 * DIFF_HEAD_DIM
    g = g_ref[...]
    for h in range(hd):
        bias = dist * (LOG2E * 2.0 ** (-8.0 * (h + 1) / hd))
        v = v_ref[:, h * dv:(h + 1) * dv]
        om = []
        for m in range(2):
            sl = slice(h * dv + m * DIFF_HEAD_DIM, h * dv + (m + 1) * DIFF_HEAD_DIM)
            om.append(_softmax_pv(_dot_nt(q_ref[:, sl], k_ref[:, sl]) - bias, v))
        o = om[0] - lam * om[1]
        o_ref[:, h * dv:(h + 1) * dv] = (_rms(o, g) * (1.0 - lam_init)).astype(BF16)


def _diff_attn(qkv, lq1, lk1, lq2, lk2, g, nb, s, dw, lam_init, tq):
    hd = dw // (2 * DIFF_HEAD_DIM)
    nq = s // tq
    vec = pl.BlockSpec((1, DIFF_HEAD_DIM), lambda b, i: (0, 0))
    kern = functools.partial(_diff_kernel, hd, tq, s, lam_init)
    return pl.pallas_call(
        kern,
        grid=(nb, nq),
        in_specs=[pl.BlockSpec((tq, dw), lambda b, i: (b * nq + i, 0)),
                  pl.BlockSpec((s, dw), lambda b, i: (b, 1)),
                  pl.BlockSpec((s, dw), lambda b, i: (b, 2)),
                  vec, vec, vec, vec,
                  pl.BlockSpec((1, 2 * DIFF_HEAD_DIM), lambda b, i: (0, 0))],
        out_specs=pl.BlockSpec((tq, dw), lambda b, i: (b * nq + i, 0)),
        out_shape=jax.ShapeDtypeStruct((nb * s, dw), BF16),
        compiler_params=_cparams(("arbitrary", "arbitrary")),
        name="diff_attn",
    )(qkv, qkv, qkv, lq1, lk1, lq2, lk2, g)


def _mla_kernel(hm, q_ref, k_ref, v_ref, o_ref):
    for h in range(hm):
        sl = slice(h * MLA_QK_PAD, (h + 1) * MLA_QK_PAD)
        o_ref[:, h * MLA_V:(h + 1) * MLA_V] = _softmax_pv(
            _dot_nt(q_ref[:, sl], k_ref[:, sl]), v_ref[:, h * MLA_V:(h + 1) * MLA_V]).astype(BF16)


def _mla_attn(qm, km, vm, nb, s, tq):
    hm = vm.shape[1] // MLA_V
    nq = s // tq
    return pl.pallas_call(
        functools.partial(_mla_kernel, hm),
        grid=(nb, nq),
        in_specs=[pl.BlockSpec((tq, hm * MLA_QK_PAD), lambda b, i: (b * nq + i, 0)),
                  pl.BlockSpec((s, hm * MLA_QK_PAD), lambda b, i: (b, 0)),
                  pl.BlockSpec((s, hm * MLA_V), lambda b, i: (b, 0))],
        out_specs=pl.BlockSpec((tq, hm * MLA_V), lambda b, i: (b * nq + i, 0)),
        out_shape=jax.ShapeDtypeStruct((nb * s, hm * MLA_V), BF16),
        compiler_params=_cparams(("arbitrary", "arbitrary")),
        name="mla_attn",
    )(qm, km, vm)


def _post_kernel(nbp, d, dw, ne, tm, alpha,
                 od_ref, om_ref, xp_ref, xs_ref, mod_ref, wo_ref, g1_ref, b1_ref, wr_ref, br_ref,
                 x1_ref, h2_ref, idx_ref, gate_ref, rank_ref, cnt_ref, run_ref):
    b = pl.program_id(0)
    first = jnp.logical_and(b == 0, pl.program_id(1) == 0)

    @pl.when(first)
    def _():
        run_ref[...] = jnp.zeros_like(run_ref)

    mix = _dot(od_ref[...], wo_ref[0:dw, :]) + _dot(om_ref[...], wo_ref[dw:2 * dw, :])
    x = jnp.where(b < nbp, xp_ref[0], xs_ref[0])
    mod = mod_ref[0]
    x1 = _ln(alpha * x + mod[:, 2 * d:3 * d] * mix) * g1_ref[...] + b1_ref[...]
    x1_ref[...] = x1
    h2 = _ln(x1) * (1.0 + mod[:, 4 * d:5 * d]) + mod[:, 3 * d:4 * d]
    h2_ref[...] = _pack_bf16_pairs(h2)
    logits = _dot(h2.astype(BF16), wr_ref[...]) + br_ref[...]

    lane = lax.broadcasted_iota(I32, (tm, ne), 1).astype(F32)
    work = logits
    vals, idxs = [], []
    for _ in range(TOP_K):
        m = jnp.max(work, axis=-1, keepdims=True)
        i = jnp.min(jnp.where(work == m, lane, float(ne)), axis=-1, keepdims=True)
        vals.append(m)
        idxs.append(i)
        work = jnp.where(lane == i, -jnp.inf, work)
    es = [jnp.exp(v - vals[0]) for v in vals]
    inv = 1.0 / (es[0] + es[1] + es[2] + es[3])

    ohs = [jnp.where(lane == i, 1.0, 0.0) for i in idxs]
    oh = ohs[0] + ohs[1] + ohs[2] + ohs[3]
    tri = jnp.where(lax.broadcasted_iota(I32, (tm, tm), 0) > lax.broadcasted_iota(I32, (tm, tm), 1),
                    1.0, 0.0)
    before = _dot(tri.astype(BF16), oh.astype(BF16)) + run_ref[...]
    run_ref[...] = run_ref[...] + jnp.sum(oh, axis=0, keepdims=True)
    cnt_ref[...] = run_ref[...]

    lane_o = lax.broadcasted_iota(I32, (tm, LANES), 1)
    idx_o = jnp.zeros((tm, LANES), I32)
    gate_o = jnp.zeros((tm, LANES), F32)
    rank_o = jnp.zeros((tm, LANES), I32)
    for k in range(TOP_K):
        rk = jnp.sum(ohs[k] * before, axis=-1, keepdims=True).astype(I32)
        idx_o = jnp.where(lane_o == k, idxs[k].astype(I32), idx_o)
        gate_o = jnp.where(lane_o == k, es[k] * inv, gate_o)
        rank_o = jnp.where(lane_o == k, rk, rank_o)
    idx_ref[...] = idx_o
    gate_ref[...] = gate_o
    rank_ref[...] = rank_o


def _post(od, om, xp, xs, mod3, wo, g1, b1, wr, br, alpha, tm):
    nbp, s, d = xp.shape
    nb = nbp + xs.shape[0]
    dw = d // 2
    ne = wr.shape[1]
    nst = s // tm
    t = nb * s
    const = lambda bb, ss: (0, 0)
    row = lambda bb, ss: (bb * nst + ss, 0)
    kern = functools.partial(_post_kernel, nbp, d, dw, ne, tm, alpha)
    return pl.pallas_call(
        kern,
        grid=(nb, nst),
        in_specs=[
            pl.BlockSpec((tm, dw), row),
            pl.BlockSpec((tm, dw), row),
            pl.BlockSpec((1, tm, d), lambda bb, ss: (jnp.minimum(bb, nbp - 1), ss, 0)),
            pl.BlockSpec((1, tm, d), lambda bb, ss: (jnp.maximum(bb - nbp, 0), ss, 0)),
            pl.BlockSpec((1, 1, 6 * d), lambda bb, ss: (bb, 0, 0)),
            pl.BlockSpec(wo.shape, const, pipeline_mode=pl.Buffered(1)),
            pl.BlockSpec((1, d), const),
            pl.BlockSpec((1, d), const),
            pl.BlockSpec(wr.shape, const),
            pl.BlockSpec((1, ne), const),
        ],
        out_specs=[
            pl.BlockSpec((tm, d), row),
            pl.BlockSpec((tm, d // 2), row),
            pl.BlockSpec((tm, LANES), row),
            pl.BlockSpec((tm, LANES), row),
            pl.BlockSpec((tm, LANES), row),
            pl.BlockSpec((1, ne), const),
        ],
        out_shape=[
            jax.ShapeDtypeStruct((t, d), F32),
            jax.ShapeDtypeStruct((t, d // 2), U32),
            jax.ShapeDtypeStruct((t, LANES), I32),
            jax.ShapeDtypeStruct((t, LANES), F32),
            jax.ShapeDtypeStruct((t, LANES), I32),
            jax.ShapeDtypeStruct((1, ne), F32),
        ],
        scratch_shapes=[pltpu.VMEM((1, ne), F32)],
        compiler_params=_cparams(("arbitrary", "arbitrary")),
        name="post_mixer",
    )(od, om, xp, xs, mod3, wo, g1, b1, wr, br)


ROW_BLOCKS_AHEAD = 2
ROW_SLOTS = ROW_BLOCKS_AHEAD + 1
ROW_DMA_PRIORITY = 1


def _expert_kernel(tm, nblk, nf,
                   be_ref, nu_ref, tok0_ref, tok1_ref, tokn_ref, src_ref, wg_ref, wu_ref, bg_ref, bu_ref,
                   w2_ref, b2_ref, o_ref, xg_ref, xb_ref, sem):
    i = pl.program_id(0)
    j = pl.program_id(1)
    n_used = nu_ref[0]
    used = i < n_used
    slot = i % ROW_SLOTS
    ahead_slot = (i + ROW_BLOCKS_AHEAD) % ROW_SLOTS
    rows_per_step = tm // nf

    def row_start(tok_ref, r, s):
        pltpu.make_async_copy(src_ref.at[pl.ds(tok_ref[r], 1)], xg_ref.at[s, pl.ds(r, 1)],
                              sem.at[s]).start(priority=ROW_DMA_PRIORITY)

    def slot_wait(s):
        pltpu.make_async_copy(src_ref.at[pl.ds(0, tm)], xg_ref.at[s], sem.at[s]).wait()

    @pl.when(jnp.logical_and(i == 0, j == 0))
    def _():
        def issue(r, carry):
            row_start(tok0_ref, r, 0)
            row_start(tok1_ref, r, 1)
            return carry
        lax.fori_loop(0, tm, issue, 0, unroll=8)

    @pl.when(jnp.logical_and(j == 0, i <= n_used + (ROW_BLOCKS_AHEAD - 1)))
    def _():
        slot_wait(slot)
        hi, lo = _unpack_bf16_pairs(xg_ref[slot])
        half = hi.shape[1]
        xb_ref[:, :half] = hi.astype(BF16)
        xb_ref[:, half:] = lo.astype(BF16)

    @pl.when(j == 0)
    def _():
        o_ref[...] = jnp.broadcast_to(b2_ref[0], o_ref.shape)

    @pl.when(used)
    def _():
        step_base = pl.multiple_of(j * rows_per_step, SUBLANES)
        for r in range(rows_per_step):
            row_start(tokn_ref, step_base + r, ahead_slot)
        xb = xb_ref[...]
        hg = _dot_nt(xb, wg_ref[0]) + bg_ref[0]
        hu = _dot_nt(xb, wu_ref[0]) + bu_ref[0]
        glu = jnp.minimum(hg, SWIGLU_LIMIT)
        lin = jnp.clip(hu, -SWIGLU_LIMIT, SWIGLU_LIMIT)
        act = glu * jax.nn.sigmoid(SWIGLU_ALPHA * glu) * (lin + 1.0)
        o_ref[...] += _dot(act.astype(BF16), w2_ref[0])

    last = jnp.logical_and(i == nblk - 1, j == nf - 1)
    filled_nblk = (n_used > nblk - 2) if nblk >= 2 else True

    @pl.when(jnp.logical_and(last, filled_nblk))
    def _():
        slot_wait(nblk % ROW_SLOTS)

    @pl.when(jnp.logical_and(last, n_used > nblk - 1))
    def _():
        slot_wait((nblk + 1) % ROW_SLOTS)


def _experts(block_e, n_used, tok, src, wgt, wut, bg, bu, w2, b2, tm, tf):
    ne, f, d = wgt.shape
    p = tok.shape[0]
    nblk = p // tm
    nf = f // tf

    def wtile(i, j, be, nu):
        return (be[i], jnp.where(i < nu[0], j, nf - 1), 0)

    def btile(i, j, be, nu):
        return (be[i], 0, jnp.where(i < nu[0], j, nf - 1))

    grid_spec = pltpu.PrefetchScalarGridSpec(
        num_scalar_prefetch=2,
        grid=(nblk, nf),
        in_specs=[
            pl.BlockSpec((tm,), lambda i, j, be, nu: (0,), memory_space=pltpu.SMEM),
            pl.BlockSpec((tm,), lambda i, j, be, nu: (min(1, nblk - 1),), memory_space=pltpu.SMEM),
            pl.BlockSpec((tm,), lambda i, j, be, nu: (jnp.minimum(i + ROW_BLOCKS_AHEAD, nblk - 1),),
                         memory_space=pltpu.SMEM),
            pl.BlockSpec(memory_space=pl.ANY),
            pl.BlockSpec((1, tf, d), wtile),
            pl.BlockSpec((1, tf, d), wtile),
            pl.BlockSpec((1, 1, tf), btile),
            pl.BlockSpec((1, 1, tf), btile),
            pl.BlockSpec((1, tf, d), wtile),
            pl.BlockSpec((1, 1, d), lambda i, j, be, nu: (be[i], 0, 0)),
        ],
        out_specs=pl.BlockSpec((tm, d), lambda i, j, be, nu: (i, 0)),
        scratch_shapes=[pltpu.VMEM((ROW_SLOTS, tm, d // 2), U32), pltpu.VMEM((tm, d), BF16),
                        pltpu.SemaphoreType.DMA((ROW_SLOTS,))],
    )
    return pl.pallas_call(
        functools.partial(_expert_kernel, tm, nblk, nf),
        grid_spec=grid_spec,
        out_shape=jax.ShapeDtypeStruct((p, d), F32),
        compiler_params=_cparams(("arbitrary", "arbitrary")),
        name="experts",
    )(block_e, n_used, tok, tok, tok, src, wgt, wut, bg, bu, w2, b2)


def _split_w1_kernel(f, x_ref, g_ref, u_ref, t_ref):
    xt = x_ref[0].T
    for c in range(t_ref.shape[0]):
        sl = slice(c * LANES, (c + 1) * LANES)
        t_ref[c] = xt[:, sl]
        g_ref[0, :, sl] = t_ref[c, pl.ds(0, f, stride=2), :].astype(BF16)
        u_ref[0, :, sl] = t_ref[c, pl.ds(1, f, stride=2), :].astype(BF16)


def _split_w1(w1, td):
    ne, d, f2 = w1.shape
    f = f2 // 2
    td = min(td, d)
    out = jax.ShapeDtypeStruct((ne, f, d), BF16)
    return pl.pallas_call(
        functools.partial(_split_w1_kernel, f),
        grid=(ne, d // td),
        in_specs=[pl.BlockSpec((1, td, f2), lambda e, i: (e, i, 0))],
        out_specs=[pl.BlockSpec((1, f, td), lambda e, i: (e, 0, i)),
                   pl.BlockSpec((1, f, td), lambda e, i: (e, 0, i))],
        out_shape=[out, out],
        scratch_shapes=[pltpu.VMEM((td // LANES, f2, LANES), F32)],
        compiler_params=_cparams(("arbitrary", "arbitrary")),
        name="split_w1",
    )(w1)


def _final_kernel(d, tm, alpha,
                  dest_ref, x1_ref, mod_ref, gate_ref, src_ref, g2_ref, b2_ref, o_ref,
                  rows_ref, sem):
    def issue(g, carry):
        base = pl.multiple_of(g * SUBLANES, SUBLANES)
        for u in range(SUBLANES):
            for k in range(TOP_K):
                pltpu.make_async_copy(src_ref.at[pl.ds(dest_ref[(base + u) * TOP_K + k], 1)],
                                      rows_ref.at[k, pl.ds(base + u, 1)], sem).start()
        return carry

    lax.fori_loop(0, tm // SUBLANES, issue, 0)
    for k in range(TOP_K):
        pltpu.make_async_copy(src_ref.at[pl.ds(0, tm)], rows_ref.at[k], sem).wait()
    gate = gate_ref[...]
    y = gate[:, 0:1] * rows_ref[0]
    for k in range(1, TOP_K):
        y = y + gate[:, k:k + 1] * rows_ref[k]
    mod = mod_ref[0]
    z = alpha * x1_ref[...] + mod[:, 5 * d:6 * d] * y
    o_ref[0] = _ln(z) * g2_ref[...] + b2_ref[...]


def _final(dest_flat, x1, mod3, gate, eo, g2, b2, alpha, b0, nbg, s, tm):
    d = x1.shape[1]
    nst = s // tm
    off = b0 * nst
    kern = functools.partial(_final_kernel, d, tm, alpha)
    return pl.pallas_call(
        kern,
        grid=(nbg * nst,),
        in_specs=[
            pl.BlockSpec((tm * TOP_K,), lambda i: (i + off,), memory_space=pltpu.SMEM),
            pl.BlockSpec((tm, d), lambda i: (i + off, 0)),
            pl.BlockSpec((1, 1, 6 * d), lambda i: (b0 + i // nst, 0, 0)),
            pl.BlockSpec((tm, LANES), lambda i: (i + off, 0)),
            pl.BlockSpec(memory_space=pl.ANY),
            pl.BlockSpec((1, d), lambda i: (0, 0)),
            pl.BlockSpec((1, d), lambda i: (0, 0)),
        ],
        out_specs=pl.BlockSpec((1, tm, d), lambda i: (i // nst, i % nst, 0)),
        out_shape=jax.ShapeDtypeStruct((nbg, s, d), F32),
        scratch_shapes=[pltpu.VMEM((TOP_K, tm, d), F32), pltpu.SemaphoreType.DMA(())],
        compiler_params=_cparams(("arbitrary",)),
        name="final_combine",
    )(dest_flat, x1, mod3, gate, eo, g2, b2)


def _rope_tables(s):
    inv = 1.0 / (ROPE_THETA ** (jnp.arange(0, MLA_ROPE, 2, dtype=F32) / MLA_ROPE))
    ang = jnp.arange(s, dtype=F32)[:, None] * inv[None, :]
    cos, sin = jnp.cos(ang), jnp.sin(ang)
    one = jnp.ones((s, MLA_NOPE), F32)
    z32 = jnp.zeros((s, ROPE_HALF), F32)
    ztail = jnp.zeros((s, MLA_QK_PAD - MLA_NOPE - MLA_ROPE), F32)
    znope = jnp.zeros((s, MLA_NOPE), F32)
    c = jnp.concatenate([one, cos, cos, ztail], axis=1)
    sa = jnp.concatenate([znope, z32, sin, ztail], axis=1)
    sb = jnp.concatenate([znope, -sin, z32, ztail], axis=1)
    return c, sa, sb


def _pad_cols(w, width):
    return jnp.pad(w, ((0, 0), (0, width - w.shape[1])))


def _layer(xp, xs, mod3, lp, layer_idx, alpha):
    nbp, s, d = xp.shape
    nb = nbp + xs.shape[0]
    dw = d // 2
    hm = (d - dw) // MLA_V
    ne = lp["w_router"].shape[1]
    f = lp["w_mlp2"].shape[1]
    t = nb * s
    tm = min(256, s)
    tq = min(512, s)

    w_in = lp["w_in"]
    q_lora = lp["mla_q_norm_g"].shape[0]
    kv_lora = lp["mla_kv_norm_g"].shape[0]
    o3, o4, o5 = 3 * dw, 3 * dw + q_lora, 3 * dw + q_lora + kv_lora
    kpe_w = jnp.concatenate([jnp.zeros((d, MLA_NOPE), F32), w_in[:, o5:],
                             jnp.zeros((d, MLA_QK_PAD - MLA_NOPE - MLA_ROPE), F32)], axis=1)
    dq_w = w_in[:, :dw] * (DIFF_HEAD_DIM ** -0.5 * LOG2E)
    win = jnp.concatenate([dq_w, w_in[:, dw:o5], kpe_w], axis=1).astype(BF16)
    wq = lp["mla_w_q_up"] * ((MLA_NOPE + MLA_ROPE) ** -0.5 * LOG2E)
    wq = wq.reshape(q_lora, hm, MLA_NOPE + MLA_ROPE)
    wq = jnp.pad(wq, ((0, 0), (0, 0), (0, MLA_QK_PAD - MLA_NOPE - MLA_ROPE)))
    wq = wq.reshape(q_lora, hm * MLA_QK_PAD).astype(BF16)
    wkv = lp["mla_w_kv_up"].reshape(kv_lora, hm, MLA_NOPE + MLA_V)
    wk = jnp.pad(wkv[:, :, :MLA_NOPE], ((0, 0), (0, 0), (0, MLA_QK_PAD - MLA_NOPE)))
    wk = wk.reshape(kv_lora, hm * MLA_QK_PAD).astype(BF16)
    wv = wkv[:, :, MLA_NOPE:].reshape(kv_lora, hm * MLA_V).astype(BF16)
    rc, rsa, rsb = _rope_tables(s)

    qkv, qm, km, vm = _ln_proj(xp, xs, mod3, win, lp["mla_q_norm_g"].reshape(1, -1), wq,
                               lp["mla_kv_norm_g"].reshape(1, -1), wk, wv, rc, rsa, rsb, tm)

    lam_init = 0.8 - 0.6 * math.exp(-0.3 * layer_idx)
    od = _diff_attn(qkv, lp["diff_lambda_q1"].reshape(1, -1), lp["diff_lambda_k1"].reshape(1, -1),
                    lp["diff_lambda_q2"].reshape(1, -1), lp["diff_lambda_k2"].reshape(1, -1),
                    lp["diff_subln_g"].reshape(1, -1), nb, s, dw, lam_init, tq)
    om = _mla_attn(qm, km, vm, nb, s, tq)

    x1, h2, idx, gate, rank, cnt = _post(
        od, om, xp, xs, mod3, lp["w_out"].astype(BF16), lp["ln1_g"].reshape(1, -1),
        lp["ln1_b"].reshape(1, -1), lp["w_router"].astype(BF16), lp["b_router"].reshape(1, -1),
        alpha, tm)

    te = min(512, t * TOP_K // ne)
    nblk = (t * TOP_K) // te + ne
    p = nblk * te
    counts = cnt[0].astype(I32)
    padded = (counts + te - 1) // te * te
    pends = jnp.cumsum(padded)
    pstarts = pends - padded
    idx4 = idx[:, :TOP_K]
    onehot = idx4[:, :, None] == jnp.arange(ne, dtype=I32)[None, None, :]
    dest = jnp.sum(jnp.where(onehot, pstarts[None, None, :], 0), axis=-1) + rank[:, :TOP_K]
    dest_flat = dest.reshape(-1).astype(I32)
    buf_tok = jnp.zeros((p,), I32).at[dest_flat].set(jnp.arange(t * TOP_K, dtype=I32) // TOP_K)
    n_used = (pends[-1] // te).astype(I32)
    blk = jnp.minimum(jnp.arange(nblk, dtype=I32), n_used - 1)
    block_e = jnp.minimum(jnp.sum((pends[None, :] <= (blk * te)[:, None]).astype(I32), axis=1), ne - 1)

    b1 = lp["b_mlp1"]
    wgt, wut = _split_w1(lp["w_mlp1"], 256)
    bg = b1[:, 0::2].reshape(ne, 1, f)
    bu = b1[:, 1::2].reshape(ne, 1, f)
    eo = _experts(block_e, n_used.reshape(1), buf_tok, h2, wgt, wut, bg, bu,
                  lp["w_mlp2"].astype(BF16), lp["b_mlp2"].reshape(ne, 1, d), te, min(1024, f))

    g2 = lp["ln2_g"].reshape(1, -1)
    b2 = lp["ln2_b"].reshape(1, -1)
    yp = _final(dest_flat, x1, mod3, gate, eo, g2, b2, alpha, 0, nbp, s, tm)
    ys = _final(dest_flat, x1, mod3, gate, eo, g2, b2, alpha, nbp, nb - nbp, s, tm)
    return yp, ys


def kernel(x_prompt, x_sample, c_prompt, c_sample, w_ada, b_ada, w_in, diff_lambda_q1, diff_lambda_k1, diff_lambda_q2, diff_lambda_k2, diff_subln_g, mla_q_norm_g, mla_w_q_up, mla_kv_norm_g, mla_w_kv_up, w_out, ln1_g, ln1_b, w_router, b_router, w_mlp1, b_mlp1, w_mlp2, b_mlp2, ln2_g, ln2_b):
    weights = dict(
        w_ada=w_ada, b_ada=b_ada, w_in=w_in, diff_lambda_q1=diff_lambda_q1,
        diff_lambda_k1=diff_lambda_k1, diff_lambda_q2=diff_lambda_q2, diff_lambda_k2=diff_lambda_k2,
        diff_subln_g=diff_subln_g, mla_q_norm_g=mla_q_norm_g, mla_w_q_up=mla_w_q_up,
        mla_kv_norm_g=mla_kv_norm_g, mla_w_kv_up=mla_w_kv_up, w_out=w_out, ln1_g=ln1_g, ln1_b=ln1_b,
        w_router=w_router, b_router=b_router, w_mlp1=w_mlp1, b_mlp1=b_mlp1, w_mlp2=w_mlp2,
        b_mlp2=b_mlp2, ln2_g=ln2_g, ln2_b=ln2_b)
    depth = w_ada.shape[0]
    assert x_prompt.shape[1:] == x_sample.shape[1:], "the two request groups are batched together"
    nbp, nbs = x_prompt.shape[0], x_sample.shape[0]
    nb = nbp + nbs
    d = x_prompt.shape[2]
    alpha = (2.0 * depth) ** 0.25
    bpad = -(-nb // 8) * 8
    c_pad = jnp.pad(jnp.concatenate([c_prompt, c_sample], axis=0), ((0, bpad - nb), (0, 0)))
    xp, xs = x_prompt, x_sample
    for i in range(depth):
        lp = {name: arr[i] for name, arr in weights.items()}
        mod3 = _ada(c_pad, lp["w_ada"], lp["b_ada"]).reshape(bpad, 1, 6 * d)
        xp, xs = _layer(xp, xs, mod3, lp, i, alpha)
    return (xp, xs)
```

```python
import functools
import math

import jax
import jax.numpy as jnp
from jax import lax
from jax.experimental import pallas as pl
from jax.experimental.pallas import tpu as pltpu

F32 = jnp.float32
BF16 = jnp.bfloat16
I32 = jnp.int32
U32 = jnp.uint32

DIFF_HEAD_DIM = 128
MLA_NOPE = 128
MLA_ROPE = 64
MLA_V = 128
ROPE_THETA = 10000.0
TOP_K = 4
SWIGLU_LIMIT = 7.0
SWIGLU_ALPHA = 1.702
LN_EPS = 1e-5
RMS_EPS = 1e-6
LOG2E = math.log2(math.e)

LANES = 128
SUBLANES = 8
MLA_QK_PAD = 2 * LANES
ROPE_HALF = MLA_ROPE // 2
VMEM_LIMIT = 56 * 1024 * 1024

ROW_TILE = 256
POST_ROW_TILE = 512
Q_TILE = 512
EXPERT_ROW_TILE = 512
EXPERT_F_TILE = 1024
SPLIT_D_TILE = 512
ADA_N_TILE = 1024


def _cparams(sem, vmem=VMEM_LIMIT):
    return pltpu.CompilerParams(dimension_semantics=sem, vmem_limit_bytes=vmem)


def _ln(x):
    mu = jnp.mean(x, axis=-1, keepdims=True)
    xc = x - mu
    var = jnp.mean(xc * xc, axis=-1, keepdims=True)
    return xc * lax.rsqrt(var + LN_EPS)


def _rms(x, g):
    return x * lax.rsqrt(jnp.mean(x * x, axis=-1, keepdims=True) + RMS_EPS) * g


def _dot(a, b):
    return jnp.dot(a, b, preferred_element_type=F32)


def _dot_nt(a, b):
    return lax.dot_general(a, b, (((1,), (1,)), ((), ())), preferred_element_type=F32)


def _pack_bf16_pairs(x):
    half = x.shape[1] // 2
    xr = x.astype(jnp.bfloat16).astype(F32)
    hi = lax.bitcast_convert_type(xr[:, :half], U32)
    lo = lax.bitcast_convert_type(xr[:, half:], U32)
    return hi | (lo >> 16)


def _unpack_bf16_pairs(p):
    hi = lax.bitcast_convert_type(p & jnp.uint32(0xFFFF0000), F32)
    lo = lax.bitcast_convert_type(p << 16, F32)
    return hi, lo


def _softmax_pv(s2, v):
    m = jnp.max(s2, axis=-1, keepdims=True)
    e = jnp.exp2(s2 - m)
    inv = 1.0 / jnp.sum(e, axis=-1, keepdims=True)
    return _dot(e.astype(BF16), v) * inv


def _ada_kernel(c_ref, w_ref, b_ref, o_ref):
    c = c_ref[...]
    s = c * jax.nn.sigmoid(c)
    o_ref[...] = _dot(s.astype(BF16), w_ref[...].astype(BF16)) + b_ref[...]


def _ada(c_pad, w_ada, b_ada):
    bp, d = c_pad.shape
    n = w_ada.shape[1]
    tn = min(ADA_N_TILE, n)
    return pl.pallas_call(
        _ada_kernel,
        grid=(n // tn,),
        in_specs=[pl.BlockSpec((bp, d), lambda j: (0, 0)),
                  pl.BlockSpec((d, tn), lambda j: (0, j)),
                  pl.BlockSpec((1, tn), lambda j: (0, j))],
        out_specs=pl.BlockSpec((bp, tn), lambda j: (0, j)),
        out_shape=jax.ShapeDtypeStruct((bp, n), F32),
        compiler_params=_cparams(("arbitrary",)),
        name="ada",
    )(c_pad, w_ada, b_ada.reshape(1, n))


def _rope_pad(x, c, sa, sb):
    return (x * c + pltpu.roll(x, ROPE_HALF, 1) * sa
            + pltpu.roll(x, MLA_QK_PAD - ROPE_HALF, 1) * sb)


def _ln_proj_kernel(nbp, d, dw, q_lora, kv_lora, hm,
                    xp_ref, xs_ref, mod_ref, win_ref, gq_ref, wq_ref, gkv_ref, wk_ref, wv_ref,
                    c_ref, sa_ref, sb_ref,
                    qkv_ref, qm_ref, km_ref, vm_ref):
    b = pl.program_id(0)
    x = jnp.where(b < nbp, xp_ref[0], xs_ref[0])
    mod = mod_ref[0]
    h = _ln(x) * (1.0 + mod[:, d:2 * d]) + mod[:, 0:d]
    hb = h.astype(BF16)
    o3 = 3 * dw
    qkv_ref[...] = _dot(hb, win_ref[:, 0:o3]).astype(BF16)
    cq = _dot(hb, win_ref[:, o3:o3 + q_lora])
    ckv = _dot(hb, win_ref[:, o3 + q_lora:o3 + q_lora + kv_lora])
    kpe = _dot(hb, win_ref[:, o3 + q_lora + kv_lora:o3 + q_lora + kv_lora + MLA_QK_PAD])
    c, sa, sb = c_ref[...], sa_ref[...], sb_ref[...]
    q = _dot(_rms(cq, gq_ref[...]).astype(BF16), wq_ref[...])
    ckvn = _rms(ckv, gkv_ref[...]).astype(BF16)
    kk = _dot(ckvn, wk_ref[...])
    vm_ref[...] = _dot(ckvn, wv_ref[...]).astype(BF16)
    kp = _rope_pad(kpe, c, sa, sb)
    for hh in range(hm):
        sl = slice(hh * MLA_QK_PAD, (hh + 1) * MLA_QK_PAD)
        qm_ref[:, sl] = _rope_pad(q[:, sl], c, sa, sb).astype(BF16)
        km_ref[:, sl] = (kk[:, sl] + kp).astype(BF16)


def _ln_proj(xp, xs, mod3, win, gq, wq, gkv, wk, wv, rc, rsa, rsb, tm):
    nbp, s, d = xp.shape
    nbs = xs.shape[0]
    nb = nbp + nbs
    dw = d // 2
    q_lora, kv_lora = gq.shape[1], gkv.shape[1]
    hm = wv.shape[1] // MLA_V
    nst = s // tm
    t = nb * s
    const = lambda bb, ss: (0, 0)
    row = lambda bb, ss: (bb * nst + ss, 0)
    once = dict(pipeline_mode=pl.Buffered(1))
    kern = functools.partial(_ln_proj_kernel, nbp, d, dw, q_lora, kv_lora, hm)
    return pl.pallas_call(
        kern,
        grid=(nb, nst),
        in_specs=[
            pl.BlockSpec((1, tm, d), lambda bb, ss: (jnp.minimum(bb, nbp - 1), ss, 0)),
            pl.BlockSpec((1, tm, d), lambda bb, ss: (jnp.maximum(bb - nbp, 0), ss, 0)),
            pl.BlockSpec((1, 1, 6 * d), lambda bb, ss: (bb, 0, 0)),
            pl.BlockSpec(win.shape, const, **once),
            pl.BlockSpec(gq.shape, const),
            pl.BlockSpec(wq.shape, const, **once),
            pl.BlockSpec(gkv.shape, const),
            pl.BlockSpec(wk.shape, const, **once),
            pl.BlockSpec(wv.shape, const, **once),
            pl.BlockSpec((tm, MLA_QK_PAD), lambda bb, ss: (ss, 0)),
            pl.BlockSpec((tm, MLA_QK_PAD), lambda bb, ss: (ss, 0)),
            pl.BlockSpec((tm, MLA_QK_PAD), lambda bb, ss: (ss, 0)),
        ],
        out_specs=[
            pl.BlockSpec((tm, 3 * dw), row),
            pl.BlockSpec((tm, hm * MLA_QK_PAD), row),
            pl.BlockSpec((tm, hm * MLA_QK_PAD), row),
            pl.BlockSpec((tm, hm * MLA_V), row),
        ],
        out_shape=[
            jax.ShapeDtypeStruct((t, 3 * dw), BF16),
            jax.ShapeDtypeStruct((t, hm * MLA_QK_PAD), BF16),
            jax.ShapeDtypeStruct((t, hm * MLA_QK_PAD), BF16),
            jax.ShapeDtypeStruct((t, hm * MLA_V), BF16),
        ],
        compiler_params=_cparams(("arbitrary", "arbitrary")),
        name="ln_proj",
    )(xp, xs, mod3, win, gq, wq, gkv, wk, wv, rc, rsa, rsb)


def _diff_kernel(hd, tq, s, lam_init,
                 q_ref, k_ref, v_ref, lq1_ref, lk1_ref, lq2_ref, lk2_ref, g_ref, o_ref):
    lam = (jnp.exp(jnp.sum(lq1_ref[...] * lk1_ref[...], axis=-1, keepdims=True))
           - jnp.exp(jnp.sum(lq2_ref[...] * lk2_ref[...], axis=-1, keepdims=True)) + lam_init)
    q0 = pl.program_id(1) * tq
    rowp = lax.broadcasted_iota(I32, (tq, s), 0) + q0
    colp = lax.broadcasted_iota(I32, (tq, s), 1)
    dist = jnp.abs(rowp - colp).astype(F32)
    dv = 2 * DIFF_HEAD_DIM
    g = g_ref[...]
    for h in range(hd):
        slope2 = LOG2E * 2.0 ** (-8.0 * (h + 1) / hd)
        v = v_ref[:, h * dv:(h + 1) * dv]
        om = []
        for m in range(2):
            sl = slice(h * dv + m * DIFF_HEAD_DIM, h * dv + (m + 1) * DIFF_HEAD_DIM)
            om.append(_softmax_pv(_dot_nt(q_ref[:, sl], k_ref[:, sl]) - dist * slope2, v))
        o = om[0] - lam * om[1]
        o_ref[:, h * dv:(h + 1) * dv] = (_rms(o, g) * (1.0 - lam_init)).astype(BF16)


def _diff_attn(qkv, lq1, lk1, lq2, lk2, g, nb, s, dw, lam_init, tq):
    hd = dw // (2 * DIFF_HEAD_DIM)
    nq = s // tq
    vec = pl.BlockSpec((1, DIFF_HEAD_DIM), lambda b, i: (0, 0))
    kern = functools.partial(_diff_kernel, hd, tq, s, lam_init)
    return pl.pallas_call(
        kern,
        grid=(nb, nq),
        in_specs=[pl.BlockSpec((tq, dw), lambda b, i: (b * nq + i, 0)),
                  pl.BlockSpec((s, dw), lambda b, i: (b, 1)),
                  pl.BlockSpec((s, dw), lambda b, i: (b, 2)),
                  vec, vec, vec, vec,
                  pl.BlockSpec((1, 2 * DIFF_HEAD_DIM), lambda b, i: (0, 0))],
        out_specs=pl.BlockSpec((tq, dw), lambda b, i: (b * nq + i, 0)),
        out_shape=jax.ShapeDtypeStruct((nb * s, dw), BF16),
        compiler_params=_cparams(("arbitrary", "arbitrary")),
        name="diff_attn",
    )(qkv, qkv, qkv, lq1, lk1, lq2, lk2, g)


def _mla_kernel(hm, q_ref, k_ref, v_ref, o_ref):
    for h in range(hm):
        sl = slice(h * MLA_QK_PAD, (h + 1) * MLA_QK_PAD)
        o_ref[:, h * MLA_V:(h + 1) * MLA_V] = _softmax_pv(
            _dot_nt(q_ref[:, sl], k_ref[:, sl]), v_ref[:, h * MLA_V:(h + 1) * MLA_V]).astype(BF16)


def _mla_attn(qm, km, vm, nb, s, tq):
    hm = vm.shape[1] // MLA_V
    nq = s // tq
    return pl.pallas_call(
        functools.partial(_mla_kernel, hm),
        grid=(nb, nq),
        in_specs=[pl.BlockSpec((tq, hm * MLA_QK_PAD), lambda b, i: (b * nq + i, 0)),
                  pl.BlockSpec((s, hm * MLA_QK_PAD), lambda b, i: (b, 0)),
                  pl.BlockSpec((s, hm * MLA_V), lambda b, i: (b, 0))],
        out_specs=pl.BlockSpec((tq, hm * MLA_V), lambda b, i: (b * nq + i, 0)),
        out_shape=jax.ShapeDtypeStruct((nb * s, hm * MLA_V), BF16),
        compiler_params=_cparams(("arbitrary", "arbitrary")),
        name="mla_attn",
    )(qm, km, vm)


def _post_kernel(nbp, d, dw, ne, tm, alpha,
                 od_ref, om_ref, xp_ref, xs_ref, mod_ref, wo_ref, g1_ref, b1_ref, wr_ref, br_ref,
                 x1_ref, h2_ref, idx_ref, gate_ref, rank_ref, cnt_ref, run_ref):
    b = pl.program_id(0)
    first = jnp.logical_and(b == 0, pl.program_id(1) == 0)

    @pl.when(first)
    def _():
        run_ref[...] = jnp.zeros_like(run_ref)

    mix = _dot(od_ref[...], wo_ref[0:dw, :]) + _dot(om_ref[...], wo_ref[dw:2 * dw, :])
    x = jnp.where(b < nbp, xp_ref[0], xs_ref[0])
    mod = mod_ref[0]
    x1 = _ln(alpha * x + mod[:, 2 * d:3 * d] * mix) * g1_ref[...] + b1_ref[...]
    x1_ref[...] = x1
    h2 = _ln(x1) * (1.0 + mod[:, 4 * d:5 * d]) + mod[:, 3 * d:4 * d]
    h2_ref[...] = _pack_bf16_pairs(h2)
    logits = _dot(h2.astype(BF16), wr_ref[...]) + br_ref[...]

    lane = lax.broadcasted_iota(I32, (tm, ne), 1).astype(F32)
    work = logits
    vals, idxs = [], []
    for _ in range(TOP_K):
        m = jnp.max(work, axis=-1, keepdims=True)
        i = jnp.min(jnp.where(work == m, lane, float(ne)), axis=-1, keepdims=True)
        vals.append(m)
        idxs.append(i)
        work = jnp.where(lane == i, -jnp.inf, work)
    es = [jnp.exp(v - vals[0]) for v in vals]
    inv = 1.0 / (es[0] + es[1] + es[2] + es[3])

    ohs = [jnp.where(lane == i, 1.0, 0.0) for i in idxs]
    oh = ohs[0] + ohs[1] + ohs[2] + ohs[3]
    tri = jnp.where(lax.broadcasted_iota(I32, (tm, tm), 0) > lax.broadcasted_iota(I32, (tm, tm), 1),
                    1.0, 0.0)
    before = _dot(tri.astype(BF16), oh.astype(BF16)) + run_ref[...]
    run_ref[...] = run_ref[...] + jnp.sum(oh, axis=0, keepdims=True)
    cnt_ref[...] = run_ref[...]

    lane_o = lax.broadcasted_iota(I32, (tm, LANES), 1)
    idx_o = jnp.zeros((tm, LANES), I32)
    gate_o = jnp.zeros((tm, LANES), F32)
    rank_o = jnp.zeros((tm, LANES), I32)
    for k in range(TOP_K):
        rk = jnp.sum(ohs[k] * before, axis=-1, keepdims=True).astype(I32)
        idx_o = jnp.where(lane_o == k, idxs[k].astype(I32), idx_o)
        gate_o = jnp.where(lane_o == k, es[k] * inv, gate_o)
        rank_o = jnp.where(lane_o == k, rk, rank_o)
    idx_ref[...] = idx_o
    gate_ref[...] = gate_o
    rank_ref[...] = rank_o


def _post(od, om, xp, xs, mod3, wo, g1, b1, wr, br, alpha, tm):
    nbp, s, d = xp.shape
    nb = nbp + xs.shape[0]
    dw = d // 2
    ne = wr.shape[1]
    nst = s // tm
    t = nb * s
    const = lambda bb, ss: (0, 0)
    row = lambda bb, ss: (bb * nst + ss, 0)
    kern = functools.partial(_post_kernel, nbp, d, dw, ne, tm, alpha)
    return pl.pallas_call(
        kern,
        grid=(nb, nst),
        in_specs=[
            pl.BlockSpec((tm, dw), row),
            pl.BlockSpec((tm, dw), row),
            pl.BlockSpec((1, tm, d), lambda bb, ss: (jnp.minimum(bb, nbp - 1), ss, 0)),
            pl.BlockSpec((1, tm, d), lambda bb, ss: (jnp.maximum(bb - nbp, 0), ss, 0)),
            pl.BlockSpec((1, 1, 6 * d), lambda bb, ss: (bb, 0, 0)),
            pl.BlockSpec(wo.shape, const, pipeline_mode=pl.Buffered(1)),
            pl.BlockSpec((1, d), const),
            pl.BlockSpec((1, d), const),
            pl.BlockSpec(wr.shape, const),
            pl.BlockSpec((1, ne), const),
        ],
        out_specs=[
            pl.BlockSpec((tm, d), row),
            pl.BlockSpec((tm, d // 2), row),
            pl.BlockSpec((tm, LANES), row),
            pl.BlockSpec((tm, LANES), row),
            pl.BlockSpec((tm, LANES), row),
            pl.BlockSpec((1, ne), const),
        ],
        out_shape=[
            jax.ShapeDtypeStruct((t, d), F32),
            jax.ShapeDtypeStruct((t, d // 2), U32),
            jax.ShapeDtypeStruct((t, LANES), I32),
            jax.ShapeDtypeStruct((t, LANES), F32),
            jax.ShapeDtypeStruct((t, LANES), I32),
            jax.ShapeDtypeStruct((1, ne), F32),
        ],
        scratch_shapes=[pltpu.VMEM((1, ne), F32)],
        compiler_params=_cparams(("arbitrary", "arbitrary")),
        name="post_mixer",
    )(od, om, xp, xs, mod3, wo, g1, b1, wr, br)


ROW_BLOCKS_AHEAD = 2
ROW_SLOTS = ROW_BLOCKS_AHEAD + 1
ROW_DMA_PRIORITY = 1


def _expert_kernel(tm, nblk, nf,
                   be_ref, nu_ref, tok0_ref, tok1_ref, tokn_ref, src_ref, wg_ref, wu_ref, bg_ref, bu_ref,
                   w2_ref, b2_ref, o_ref, xg_ref, xb_ref, sem):
    i = pl.program_id(0)
    j = pl.program_id(1)
    n_used = nu_ref[0]
    used = i < n_used
    slot = i % ROW_SLOTS
    ahead_slot = (i + ROW_BLOCKS_AHEAD) % ROW_SLOTS
    rows_per_step = tm // nf

    def row_start(tok_ref, r, s):
        pltpu.make_async_copy(src_ref.at[pl.ds(tok_ref[r], 1)], xg_ref.at[s, pl.ds(r, 1)],
                              sem.at[s]).start(priority=ROW_DMA_PRIORITY)

    def slot_wait(s):
        pltpu.make_async_copy(src_ref.at[pl.ds(0, tm)], xg_ref.at[s], sem.at[s]).wait()

    @pl.when(jnp.logical_and(i == 0, j == 0))
    def _():
        def issue(r, carry):
            row_start(tok0_ref, r, 0)
            row_start(tok1_ref, r, 1)
            return carry
        lax.fori_loop(0, tm, issue, 0, unroll=8)

    @pl.when(jnp.logical_and(j == 0, i <= n_used + (ROW_BLOCKS_AHEAD - 1)))
    def _():
        slot_wait(slot)
        hi, lo = _unpack_bf16_pairs(xg_ref[slot])
        half = hi.shape[1]
        xb_ref[:, :half] = hi.astype(BF16)
        xb_ref[:, half:] = lo.astype(BF16)

    @pl.when(j == 0)
    def _():
        o_ref[...] = jnp.broadcast_to(b2_ref[0], o_ref.shape)

    @pl.when(used)
    def _():
        step_base = pl.multiple_of(j * rows_per_step, SUBLANES)
        for r in range(rows_per_step):
            row_start(tokn_ref, step_base + r, ahead_slot)
        xb = xb_ref[...]
        hg = _dot_nt(xb, wg_ref[0]) + bg_ref[0]
        hu = _dot_nt(xb, wu_ref[0]) + bu_ref[0]
        glu = jnp.minimum(hg, SWIGLU_LIMIT)
        lin = jnp.clip(hu, -SWIGLU_LIMIT, SWIGLU_LIMIT)
        act = glu * jax.nn.sigmoid(SWIGLU_ALPHA * glu) * (lin + 1.0)
        o_ref[...] += _dot(act.astype(BF16), w2_ref[0].astype(BF16))

    last = jnp.logical_and(i == nblk - 1, j == nf - 1)
    filled_nblk = (n_used > nblk - 2) if nblk >= 2 else True

    @pl.when(jnp.logical_and(last, filled_nblk))
    def _():
        slot_wait(nblk % ROW_SLOTS)

    @pl.when(jnp.logical_and(last, n_used > nblk - 1))
    def _():
        slot_wait((nblk + 1) % ROW_SLOTS)


def _experts(block_e, n_used, tok, src, wgt, wut, bg, bu, w2, b2, tm, tf):
    ne, f, d = wgt.shape
    p = tok.shape[0]
    nblk = p // tm
    nf = f // tf

    def wtile(i, j, be, nu):
        return (be[i], jnp.where(i < nu[0], j, nf - 1), 0)

    def btile(i, j, be, nu):
        return (be[i], 0, jnp.where(i < nu[0], j, nf - 1))

    grid_spec = pltpu.PrefetchScalarGridSpec(
        num_scalar_prefetch=2,
        grid=(nblk, nf),
        in_specs=[
            pl.BlockSpec((tm,), lambda i, j, be, nu: (0,), memory_space=pltpu.SMEM),
            pl.BlockSpec((tm,), lambda i, j, be, nu: (min(1, nblk - 1),), memory_space=pltpu.SMEM),
            pl.BlockSpec((tm,), lambda i, j, be, nu: (jnp.minimum(i + ROW_BLOCKS_AHEAD, nblk - 1),),
                         memory_space=pltpu.SMEM),
            pl.BlockSpec(memory_space=pl.ANY),
            pl.BlockSpec((1, tf, d), wtile),
            pl.BlockSpec((1, tf, d), wtile),
            pl.BlockSpec((1, 1, tf), btile),
            pl.BlockSpec((1, 1, tf), btile),
            pl.BlockSpec((1, tf, d), wtile),
            pl.BlockSpec((1, 1, d), lambda i, j, be, nu: (be[i], 0, 0)),
        ],
        out_specs=pl.BlockSpec((tm, d), lambda i, j, be, nu: (i, 0)),
        scratch_shapes=[pltpu.VMEM((ROW_SLOTS, tm, d // 2), U32), pltpu.VMEM((tm, d), BF16),
                        pltpu.SemaphoreType.DMA((ROW_SLOTS,))],
    )
    return pl.pallas_call(
        functools.partial(_expert_kernel, tm, nblk, nf),
        grid_spec=grid_spec,
        out_shape=jax.ShapeDtypeStruct((p, d), F32),
        compiler_params=_cparams(("arbitrary", "arbitrary")),
        name="experts",
    )(block_e, n_used, tok, tok, tok, src, wgt, wut, bg, bu, w2, b2)


def _split_w1_kernel(f, x_ref, g_ref, u_ref, t_ref):
    xt = x_ref[0].T
    for c in range(t_ref.shape[0]):
        sl = slice(c * LANES, (c + 1) * LANES)
        t_ref[c] = xt[:, sl]
        g_ref[0, :, sl] = t_ref[c, pl.ds(0, f, stride=2), :].astype(BF16)
        u_ref[0, :, sl] = t_ref[c, pl.ds(1, f, stride=2), :].astype(BF16)


def _split_w1(w1, td):
    ne, d, f2 = w1.shape
    f = f2 // 2
    td = min(td, d)
    out = jax.ShapeDtypeStruct((ne, f, d), BF16)
    return pl.pallas_call(
        functools.partial(_split_w1_kernel, f),
        grid=(ne, d // td),
        in_specs=[pl.BlockSpec((1, td, f2), lambda e, i: (e, i, 0))],
        out_specs=[pl.BlockSpec((1, f, td), lambda e, i: (e, 0, i)),
                   pl.BlockSpec((1, f, td), lambda e, i: (e, 0, i))],
        out_shape=[out, out],
        scratch_shapes=[pltpu.VMEM((td // LANES, f2, LANES), F32)],
        compiler_params=_cparams(("arbitrary", "arbitrary")),
        name="split_w1",
    )(w1)


def _final_kernel(d, tm, nsteps, alpha,
                  dest0_ref, destn_ref, x1_ref, mod_ref, gate_ref, src_ref, g2_ref, b2_ref, o_ref,
                  rows_ref, sem):
    i = pl.program_id(0)
    slot = i % 2

    def issue_rows(dest_ref, s):
        def issue(g, carry):
            base = pl.multiple_of(g * SUBLANES, SUBLANES)
            for u in range(SUBLANES):
                for k in range(TOP_K):
                    pltpu.make_async_copy(src_ref.at[pl.ds(dest_ref[(base + u) * TOP_K + k], 1)],
                                          rows_ref.at[s, k, pl.ds(base + u, 1)], sem.at[s]).start()
            return carry
        lax.fori_loop(0, tm // SUBLANES, issue, 0)

    @pl.when(i == 0)
    def _():
        issue_rows(dest0_ref, 0)

    @pl.when(i + 1 < nsteps)
    def _():
        issue_rows(destn_ref, 1 - slot)

    for k in range(TOP_K):
        pltpu.make_async_copy(src_ref.at[pl.ds(0, tm)], rows_ref.at[slot, k], sem.at[slot]).wait()
    gate = gate_ref[...]
    y = gate[:, 0:1] * rows_ref[slot, 0]
    for k in range(1, TOP_K):
        y = y + gate[:, k:k + 1] * rows_ref[slot, k]
    mod = mod_ref[0]
    z = alpha * x1_ref[...] + mod[:, 5 * d:6 * d] * y
    o_ref[0] = _ln(z) * g2_ref[...] + b2_ref[...]


def _final(dest_flat, x1, mod3, gate, eo, g2, b2, alpha, b0, nbg, s, tm):
    d = x1.shape[1]
    nst = s // tm
    off = b0 * nst
    nsteps = nbg * nst
    kern = functools.partial(_final_kernel, d, tm, nsteps, alpha)
    return pl.pallas_call(
        kern,
        grid=(nsteps,),
        in_specs=[
            pl.BlockSpec((tm * TOP_K,), lambda i: (off,), memory_space=pltpu.SMEM),
            pl.BlockSpec((tm * TOP_K,), lambda i: (jnp.minimum(i + 1, nsteps - 1) + off,),
                         memory_space=pltpu.SMEM),
            pl.BlockSpec((tm, d), lambda i: (i + off, 0)),
            pl.BlockSpec((1, 1, 6 * d), lambda i: (b0 + i // nst, 0, 0)),
            pl.BlockSpec((tm, LANES), lambda i: (i + off, 0)),
            pl.BlockSpec(memory_space=pl.ANY),
            pl.BlockSpec((1, d), lambda i: (0, 0)),
            pl.BlockSpec((1, d), lambda i: (0, 0)),
        ],
        out_specs=pl.BlockSpec((1, tm, d), lambda i: (i // nst, i % nst, 0)),
        out_shape=jax.ShapeDtypeStruct((nbg, s, d), F32),
        scratch_shapes=[pltpu.VMEM((2, TOP_K, tm, d), F32), pltpu.SemaphoreType.DMA((2,))],
        compiler_params=_cparams(("arbitrary",)),
        name="final_combine",
    )(dest_flat, dest_flat, x1, mod3, gate, eo, g2, b2)


def _rope_tables(s):
    inv = 1.0 / (ROPE_THETA ** (jnp.arange(0, MLA_ROPE, 2, dtype=F32) / MLA_ROPE))
    ang = jnp.arange(s, dtype=F32)[:, None] * inv[None, :]
    cos, sin = jnp.cos(ang), jnp.sin(ang)
    one = jnp.ones((s, MLA_NOPE), F32)
    z32 = jnp.zeros((s, ROPE_HALF), F32)
    ztail = jnp.zeros((s, MLA_QK_PAD - MLA_NOPE - MLA_ROPE), F32)
    znope = jnp.zeros((s, MLA_NOPE), F32)
    c = jnp.concatenate([one, cos, cos, ztail], axis=1)
    sa = jnp.concatenate([znope, z32, sin, ztail], axis=1)
    sb = jnp.concatenate([znope, -sin, z32, ztail], axis=1)
    return c, sa, sb


def _pad_cols(w, width):
    return jnp.pad(w, ((0, 0), (0, width - w.shape[1])))


def _layer(xp, xs, mod3, lp, layer_idx, alpha):
    nbp, s, d = xp.shape
    nb = nbp + xs.shape[0]
    dw = d // 2
    hm = (d - dw) // MLA_V
    ne = lp["w_router"].shape[1]
    f = lp["w_mlp2"].shape[1]
    t = nb * s
    tm = min(ROW_TILE, s)
    tq = min(Q_TILE, s)

    w_in = lp["w_in"]
    q_lora = lp["mla_q_norm_g"].shape[0]
    kv_lora = lp["mla_kv_norm_g"].shape[0]
    o3, o4, o5 = 3 * dw, 3 * dw + q_lora, 3 * dw + q_lora + kv_lora
    kpe_w = jnp.concatenate([jnp.zeros((d, MLA_NOPE), F32), w_in[:, o5:],
                             jnp.zeros((d, MLA_QK_PAD - MLA_NOPE - MLA_ROPE), F32)], axis=1)
    dq_w = w_in[:, :dw] * (DIFF_HEAD_DIM ** -0.5 * LOG2E)
    win = jnp.concatenate([dq_w, w_in[:, dw:o5], kpe_w], axis=1).astype(BF16)
    wq = lp["mla_w_q_up"] * ((MLA_NOPE + MLA_ROPE) ** -0.5 * LOG2E)
    wq = wq.reshape(q_lora, hm, MLA_NOPE + MLA_ROPE)
    wq = jnp.pad(wq, ((0, 0), (0, 0), (0, MLA_QK_PAD - MLA_NOPE - MLA_ROPE)))
    wq = wq.reshape(q_lora, hm * MLA_QK_PAD).astype(BF16)
    wkv = lp["mla_w_kv_up"].reshape(kv_lora, hm, MLA_NOPE + MLA_V)
    wk = jnp.pad(wkv[:, :, :MLA_NOPE], ((0, 0), (0, 0), (0, MLA_QK_PAD - MLA_NOPE)))
    wk = wk.reshape(kv_lora, hm * MLA_QK_PAD).astype(BF16)
    wv = wkv[:, :, MLA_NOPE:].reshape(kv_lora, hm * MLA_V).astype(BF16)
    rc, rsa, rsb = _rope_tables(s)

    qkv, qm, km, vm = _ln_proj(xp, xs, mod3, win, lp["mla_q_norm_g"].reshape(1, -1), wq,
                               lp["mla_kv_norm_g"].reshape(1, -1), wk, wv, rc, rsa, rsb, tm)

    lam_init = 0.8 - 0.6 * math.exp(-0.3 * layer_idx)
    od = _diff_attn(qkv, lp["diff_lambda_q1"].reshape(1, -1), lp["diff_lambda_k1"].reshape(1, -1),
                    lp["diff_lambda_q2"].reshape(1, -1), lp["diff_lambda_k2"].reshape(1, -1),
                    lp["diff_subln_g"].reshape(1, -1), nb, s, dw, lam_init, tq)
    om = _mla_attn(qm, km, vm, nb, s, tq)

    x1, h2, idx, gate, rank, cnt = _post(
        od, om, xp, xs, mod3, lp["w_out"].astype(BF16), lp["ln1_g"].reshape(1, -1),
        lp["ln1_b"].reshape(1, -1), lp["w_router"].astype(BF16), lp["b_router"].reshape(1, -1),
        alpha, min(POST_ROW_TILE, s))

    te = min(EXPERT_ROW_TILE, t * TOP_K // ne)
    nblk = (t * TOP_K) // te + ne
    p = nblk * te
    counts = cnt[0].astype(I32)
    padded = (counts + te - 1) // te * te
    pends = jnp.cumsum(padded)
    pstarts = pends - padded
    idx4 = idx[:, :TOP_K]
    onehot = idx4[:, :, None] == jnp.arange(ne, dtype=I32)[None, None, :]
    dest = jnp.sum(jnp.where(onehot, pstarts[None, None, :], 0), axis=-1) + rank[:, :TOP_K]
    dest_flat = dest.reshape(-1).astype(I32)
    buf_tok = jnp.zeros((p,), I32).at[dest_flat].set(
        jnp.arange(t * TOP_K, dtype=I32) // TOP_K, unique_indices=True, mode="promise_in_bounds")
    n_used = (pends[-1] // te).astype(I32)
    blk = jnp.minimum(jnp.arange(nblk, dtype=I32), n_used - 1)
    block_e = jnp.minimum(jnp.sum((pends[None, :] <= (blk * te)[:, None]).astype(I32), axis=1), ne - 1)

    b1 = lp["b_mlp1"]
    wgt, wut = _split_w1(lp["w_mlp1"], SPLIT_D_TILE)
    bg = b1[:, 0::2].reshape(ne, 1, f)
    bu = b1[:, 1::2].reshape(ne, 1, f)
    eo = _experts(block_e, n_used.reshape(1), buf_tok, h2, wgt, wut, bg, bu,
                  lp["w_mlp2"], lp["b_mlp2"].reshape(ne, 1, d), te, min(EXPERT_F_TILE, f))

    g2 = lp["ln2_g"].reshape(1, -1)
    b2 = lp["ln2_b"].reshape(1, -1)
    yp = _final(dest_flat, x1, mod3, gate, eo, g2, b2, alpha, 0, nbp, s, tm)
    ys = _final(dest_flat, x1, mod3, gate, eo, g2, b2, alpha, nbp, nb - nbp, s, tm)
    return yp, ys


def kernel(x_prompt, x_sample, c_prompt, c_sample, w_ada, b_ada, w_in, diff_lambda_q1, diff_lambda_k1, diff_lambda_q2, diff_lambda_k2, diff_subln_g, mla_q_norm_g, mla_w_q_up, mla_kv_norm_g, mla_w_kv_up, w_out, ln1_g, ln1_b, w_router, b_router, w_mlp1, b_mlp1, w_mlp2, b_mlp2, ln2_g, ln2_b):
    weights = dict(
        w_ada=w_ada, b_ada=b_ada, w_in=w_in, diff_lambda_q1=diff_lambda_q1,
        diff_lambda_k1=diff_lambda_k1, diff_lambda_q2=diff_lambda_q2, diff_lambda_k2=diff_lambda_k2,
        diff_subln_g=diff_subln_g, mla_q_norm_g=mla_q_norm_g, mla_w_q_up=mla_w_q_up,
        mla_kv_norm_g=mla_kv_norm_g, mla_w_kv_up=mla_w_kv_up, w_out=w_out, ln1_g=ln1_g, ln1_b=ln1_b,
        w_router=w_router, b_router=b_router, w_mlp1=w_mlp1, b_mlp1=b_mlp1, w_mlp2=w_mlp2,
        b_mlp2=b_mlp2, ln2_g=ln2_g, ln2_b=ln2_b)
    depth = w_ada.shape[0]
    assert x_prompt.shape[1:] == x_sample.shape[1:], "the two request groups are batched together"
    nbp, nbs = x_prompt.shape[0], x_sample.shape[0]
    nb = nbp + nbs
    d = x_prompt.shape[2]
    alpha = (2.0 * depth) ** 0.25
    bpad = -(-nb // 8) * 8
    c_pad = jnp.pad(jnp.concatenate([c_prompt, c_sample], axis=0), ((0, bpad - nb), (0, 0)))
    xp, xs = x_prompt, x_sample
    for i in range(depth):
        lp = {name: arr[i] for name, arr in weights.items()}
        mod3 = _ada(c_pad, lp["w_ada"], lp["b_ada"]).reshape(bpad, 1, 6 * d)
        xp, xs = _layer(xp, xs, mod3, lp, i, alpha)
    return (xp, xs)
```
